```python
import jax, jax.numpy as jnp
from jax import lax
import numpy as np

D_MODEL = 4096
BATCH = 4
SEQ = 2048
DEPTH = 2
DEC_BATCH = 128
DEC_SEQ = 1
PAST_LEN = 16384
PAGE_SIZE = 128

D_CONV = D_MODEL
SCONV_W = 3
D_INNER = 2 * D_MODEL
SSD_HEAD_DIM = 64
SSD_HEADS = D_INNER // SSD_HEAD_DIM
SSD_GROUPS = 8
SSD_HPG = SSD_HEADS // SSD_GROUPS
SSD_STATE = 128
SSD_CONV_W = 4
SSD_CONV_CH = D_INNER + 2 * SSD_GROUPS * SSD_STATE
SSD_CHUNK = 128
N_MEM = 256
XA_HEADS = 4
XA_HEAD_DIM = 128
XA_WIDTH = XA_HEADS * XA_HEAD_DIM
N_EXPERTS = 32
N_EXPERT_GROUPS = 8
EXPERTS_PER_GROUP = N_EXPERTS // N_EXPERT_GROUPS
TOP_K = 2
D_EXPERT = D_MODEL // 4
EPS = 1e-6
IN_SPLITS = (D_CONV, 2 * D_CONV, 3 * D_CONV, 3 * D_CONV + D_INNER, 3 * D_CONV + D_INNER + SSD_CONV_CH, 3 * D_CONV + D_INNER + SSD_CONV_CH + SSD_HEADS, 3 * D_CONV + D_INNER + SSD_CONV_CH + SSD_HEADS + D_MODEL)
IN_COLS = 3 * D_CONV + D_INNER + SSD_CONV_CH + SSD_HEADS + 2 * D_MODEL

kernel_name = "hybrid_shortconv_ssd_memxattn_groupmoe_step"


def rmsnorm(x, g):
    xf = x.astype(jnp.float32)
    xf = xf * lax.rsqrt(jnp.mean(xf * xf, axis=-1, keepdims=True) + EPS)
    return xf.astype(x.dtype) * g


def gated_rmsnorm(y, z, g):
    v = (y * jax.nn.silu(z)).astype(jnp.float32)
    v = v.reshape(*y.shape[:-1], SSD_GROUPS, D_INNER // SSD_GROUPS)
    v = v * lax.rsqrt(jnp.mean(v * v, axis=-1, keepdims=True) + EPS)
    return v.reshape(y.shape).astype(y.dtype) * g


def causal_dwconv(xp, w, length):
    return sum(xp[:, k:k + length] * w[k] for k in range(w.shape[0]))


def ssd_scan(x, dt, a, bm, cm, h0):
    b, L = x.shape[:2]
    q = SSD_CHUNK if L % SSD_CHUNK == 0 else L
    c = L // q
    dtype = x.dtype
    x = x.reshape(b, c, q, *x.shape[2:])
    dt = dt.reshape(b, c, q, *dt.shape[2:])
    bm = bm.reshape(b, c, q, *bm.shape[2:])
    cm = cm.reshape(b, c, q, *cm.shape[2:])
    a_cum = jnp.cumsum(dt * a, axis=2)
    ac = jnp.moveaxis(a_cum, 2, -1)
    causal = jnp.tril(jnp.ones((q, q), dtype=bool))
    seg = jnp.where(causal, ac[..., :, None] - ac[..., None, :], -jnp.inf)
    decay_in = jnp.exp(seg).astype(dtype)
    xdt = x * dt.astype(dtype)[..., None]
    cb = jnp.einsum('bclgn,bcsgn->bcgls', cm, bm)
    y_diag = jnp.einsum('bcgls,bcghls,bcsghp->bclghp', cb, decay_in, xdt)
    decay_to_end = jnp.exp(a_cum[:, :, -1:] - a_cum).astype(dtype)
    chunk_states = jnp.einsum('bcsgn,bcsgh,bcsghp->bcghpn', bm, decay_to_end, xdt)
    chunk_decay = jnp.exp(a_cum[:, :, -1]).astype(h0.dtype)

    def step(h, inp):
        s, d = inp
        return d[..., None, None] * h + s.astype(h.dtype), h

    h_final, h_prev = lax.scan(step, h0, (jnp.moveaxis(chunk_states, 1, 0), jnp.moveaxis(chunk_decay, 1, 0)))
    h_prev = jnp.moveaxis(h_prev, 0, 1).astype(dtype)
    y_off = jnp.einsum('bclgn,bcghpn,bclgh->bclghp', cm, h_prev, jnp.exp(a_cum).astype(dtype))
    return (y_diag + y_off).reshape(b, L, *x.shape[3:]), h_final


def mixer_block(xn, sconv_buf, ssm_conv_buf, h0, w_in, sconv_w, w_sconv_out, ssd_conv_w, ssd_conv_b,
                ssd_dt_bias, ssd_a_log, ssd_d, ssd_norm, w_ssd_out, w_mix_out):
    b, L, _ = xn.shape
    proj = xn @ w_in
    c_b, c_c, c_x, z, xbc_raw, dt, g_conv, g_ssd = jnp.split(proj, IN_SPLITS, axis=-1)
    u_pad = jnp.concatenate([sconv_buf.astype(xn.dtype), c_c * c_x], axis=1)
    y_conv = (c_b * causal_dwconv(u_pad, sconv_w, L)) @ w_sconv_out
    xbc_pad = jnp.concatenate([ssm_conv_buf.astype(xn.dtype), xbc_raw], axis=1)
    xbc = jax.nn.silu(causal_dwconv(xbc_pad, ssd_conv_w, L) + ssd_conv_b)
    xs, bm, cm = jnp.split(xbc, [D_INNER, D_INNER + SSD_GROUPS * SSD_STATE], axis=-1)
    xs = xs.reshape(b, L, SSD_GROUPS, SSD_HPG, SSD_HEAD_DIM)
    bm = bm.reshape(b, L, SSD_GROUPS, SSD_STATE)
    cm = cm.reshape(b, L, SSD_GROUPS, SSD_STATE)
    dt = jax.nn.softplus((dt + ssd_dt_bias).astype(jnp.float32)).reshape(b, L, SSD_GROUPS, SSD_HPG)
    a = -jnp.exp(ssd_a_log.astype(jnp.float32)).reshape(SSD_GROUPS, SSD_HPG)
    y, h_new = ssd_scan(xs, dt, a, bm, cm, h0)
    y = y + ssd_d.reshape(SSD_GROUPS, SSD_HPG, 1) * xs
    y_ssd = gated_rmsnorm(y.reshape(b, L, D_INNER), z, ssd_norm) @ w_ssd_out
    mixed = jax.nn.sigmoid(g_conv) * y_conv + jax.nn.sigmoid(g_ssd) * y_ssd
    return mixed @ w_mix_out, u_pad[:, -(SCONV_W - 1):], xbc_pad[:, -(SSD_CONV_W - 1):], h_new


def memory_kv(mem, norm_mem, w_xk, w_xv):
    b, m, _ = mem.shape
    memn = rmsnorm(mem, norm_mem)
    k = (memn @ w_xk).reshape(b, m, XA_HEADS, XA_HEAD_DIM)
    v = (memn @ w_xv).reshape(b, m, XA_HEADS, XA_HEAD_DIM)
    return k, v


def memory_xattn(xn, k, v, w_xq, w_xo):
    b, L, _ = xn.shape
    q = (xn @ w_xq).reshape(b, L, XA_HEADS, XA_HEAD_DIM)
    s = jnp.einsum('blhd,bmhd->bhlm', q, k.astype(q.dtype)).astype(jnp.float32) * (XA_HEAD_DIM ** -0.5)
    p = jax.nn.softmax(s, axis=-1).astype(q.dtype)
    o = jnp.einsum('bhlm,bmhd->blhd', p, v.astype(q.dtype)).reshape(b, L, XA_WIDTH)
    return o @ w_xo


def group_limited_moe(xn, w_router, router_bias, w_gate, w_up, w_down):
    scores = jax.nn.sigmoid((xn @ w_router).astype(jnp.float32))
    biased = scores + router_bias.astype(jnp.float32)
    grouped = biased.reshape(*biased.shape[:-1], N_EXPERT_GROUPS, EXPERTS_PER_GROUP)
    group_score = jnp.sum(lax.top_k(grouped, 2)[0], axis=-1)
    top_group = jnp.argmax(group_score, axis=-1)
    in_group = top_group[..., None] == jnp.arange(N_EXPERT_GROUPS)
    masked = jnp.where(in_group[..., None], grouped, -jnp.inf).reshape(biased.shape)
    _, idx = lax.top_k(masked, TOP_K)
    sel = jnp.take_along_axis(scores, idx, axis=-1)
    wts = sel / jnp.sum(sel, axis=-1, keepdims=True)
    gates = jnp.sum((idx[..., None] == jnp.arange(N_EXPERTS)) * wts[..., None], axis=-2).astype(xn.dtype)
    h = jax.nn.silu(jnp.einsum('bld,edf->blef', xn, w_gate)) * jnp.einsum('bld,edf->blef', xn, w_up)
    return jnp.einsum('blef,efd->bld', h * gates[..., None], w_down)


def decoder_layer(x, mem_k, mem_v, sconv_buf, ssm_conv_buf, h0, norm_mix, w_in, sconv_w, w_sconv_out,
                  ssd_conv_w, ssd_conv_b, ssd_dt_bias, ssd_a_log, ssd_d, ssd_norm, w_ssd_out, w_mix_out,
                  norm_xattn, w_xq, w_xo, norm_moe, w_router, router_bias, w_gate, w_up, w_down):
    m, sb, cb, h = mixer_block(rmsnorm(x, norm_mix), sconv_buf, ssm_conv_buf, h0, w_in, sconv_w, w_sconv_out,
                               ssd_conv_w, ssd_conv_b, ssd_dt_bias, ssd_a_log, ssd_d, ssd_norm, w_ssd_out, w_mix_out)
    x = x + m
    x = x + memory_xattn(rmsnorm(x, norm_xattn), mem_k, mem_v, w_xq, w_xo)
    x = x + group_limited_moe(rmsnorm(x, norm_moe), w_router, router_bias, w_gate, w_up, w_down)
    return x, sb, cb, h


def setup_inputs(seed: int = 0) -> dict:
    key = jax.random.key(seed)
    ks = iter(jax.random.split(key, 40))
    f32 = jnp.float32

    def nrm(shape, scale):
        return jax.random.normal(next(ks), shape, f32) * scale

    def gain(shape):
        return 1.0 + 0.02 * jax.random.normal(next(ks), shape, f32)

    dt0 = jnp.exp(jax.random.uniform(next(ks), (DEPTH, SSD_HEADS), f32) * (np.log(0.1) - np.log(0.001)) + np.log(0.001))
    return {
        "x_prompt": nrm((BATCH, SEQ, D_MODEL), 1.0),
        "x_sample": nrm((DEC_BATCH, DEC_SEQ, D_MODEL), 1.0),
        "mem_prompt": nrm((BATCH, N_MEM, D_MODEL), 1.0),
        "cache_mem_k": nrm((DEPTH, DEC_BATCH, N_MEM, XA_HEADS, XA_HEAD_DIM), 1.0),
        "cache_mem_v": nrm((DEPTH, DEC_BATCH, N_MEM, XA_HEADS, XA_HEAD_DIM), 1.0),
        "state_sconv": nrm((DEPTH, DEC_BATCH, SCONV_W - 1, D_CONV), 0.5),
        "state_ssm_conv": nrm((DEPTH, DEC_BATCH, SSD_CONV_W - 1, SSD_CONV_CH), 0.5),
        "state_ssm": nrm((DEPTH, DEC_BATCH, SSD_HEADS, SSD_HEAD_DIM, SSD_STATE), 0.5),
        "norm_mix": gain((DEPTH, D_MODEL)),
        "w_in": nrm((DEPTH, D_MODEL, IN_COLS), D_MODEL ** -0.5),
        "sconv_w": nrm((DEPTH, SCONV_W, D_CONV), SCONV_W ** -0.5),
        "w_sconv_out": nrm((DEPTH, D_CONV, D_MODEL), D_CONV ** -0.5),
        "ssd_conv_w": nrm((DEPTH, SSD_CONV_W, SSD_CONV_CH), SSD_CONV_W ** -0.5),
        "ssd_conv_b": nrm((DEPTH, SSD_CONV_CH), 0.02),
        "ssd_dt_bias": jnp.log(jnp.expm1(dt0)),
        "ssd_a_log": jnp.log(jax.random.uniform(next(ks), (DEPTH, SSD_HEADS), f32, 1.0, 16.0)),
        "ssd_d": gain((DEPTH, SSD_HEADS)),
        "ssd_norm": gain((DEPTH, D_INNER)),
        "w_ssd_out": nrm((DEPTH, D_INNER, D_MODEL), D_INNER ** -0.5),
        "w_mix_out": nrm((DEPTH, D_MODEL, D_MODEL), 0.5 * D_MODEL ** -0.5),
        "norm_xattn": gain((DEPTH, D_MODEL)),
        "norm_mem": gain((DEPTH, D_MODEL)),
        "w_xk": nrm((DEPTH, D_MODEL, XA_WIDTH), D_MODEL ** -0.5),
        "w_xv": nrm((DEPTH, D_MODEL, XA_WIDTH), D_MODEL ** -0.5),
        "w_xq": nrm((DEPTH, D_MODEL, XA_WIDTH), D_MODEL ** -0.5),
        "w_xo": nrm((DEPTH, XA_WIDTH, D_MODEL), 0.5 * XA_WIDTH ** -0.5),
        "norm_moe": gain((DEPTH, D_MODEL)),
        "w_router": nrm((D_MODEL, N_EXPERTS), D_MODEL ** -0.5),
        "router_bias": nrm((N_EXPERTS,), 0.01),
        "w_gate": nrm((DEPTH, N_EXPERTS, D_MODEL, D_EXPERT), D_MODEL ** -0.5),
        "w_up": nrm((DEPTH, N_EXPERTS, D_MODEL, D_EXPERT), D_MODEL ** -0.5),
        "w_down": nrm((DEPTH, N_EXPERTS, D_EXPERT, D_MODEL), 0.5 * D_EXPERT ** -0.5),
        "norm_final": gain((D_MODEL,)),
    }


def reference(x_prompt, x_sample, mem_prompt, cache_mem_k, cache_mem_v, state_sconv, state_ssm_conv, state_ssm,
              norm_mix, w_in, sconv_w, w_sconv_out, ssd_conv_w, ssd_conv_b, ssd_dt_bias, ssd_a_log, ssd_d, ssd_norm,
              w_ssd_out, w_mix_out, norm_xattn, norm_mem, w_xk, w_xv, w_xq, w_xo, norm_moe, w_router, router_bias,
              w_gate, w_up, w_down, norm_final):
    bp = x_prompt.shape[0]
    bs = x_sample.shape[0]
    dtp = x_prompt.dtype
    xp, xs = x_prompt, x_sample
    mk_p, mv_p, sc_p, cc_p, h_p, sc_s, cc_s, h_s = [], [], [], [], [], [], [], []
    for l in range(DEPTH):
        lw = (norm_mix[l], w_in[l], sconv_w[l], w_sconv_out[l], ssd_conv_w[l], ssd_conv_b[l], ssd_dt_bias[l],
              ssd_a_log[l], ssd_d[l], ssd_norm[l], w_ssd_out[l], w_mix_out[l], norm_xattn[l], w_xq[l], w_xo[l],
              norm_moe[l], w_router, router_bias, w_gate[l], w_up[l], w_down[l])
        k_p, v_p = memory_kv(mem_prompt, norm_mem[l], w_xk[l], w_xv[l])
        xp, sb, cb, h = decoder_layer(
            xp, k_p, v_p,
            jnp.zeros((bp, SCONV_W - 1, D_CONV), dtp),
            jnp.zeros((bp, SSD_CONV_W - 1, SSD_CONV_CH), dtp),
            jnp.zeros((bp, SSD_GROUPS, SSD_HPG, SSD_HEAD_DIM, SSD_STATE), dtp),
            *lw)
        mk_p.append(k_p)
        mv_p.append(v_p)
        sc_p.append(sb)
        cc_p.append(cb)
        h_p.append(h.reshape(bp, SSD_HEADS, SSD_HEAD_DIM, SSD_STATE))
        xs, sb, cb, h = decoder_layer(
            xs, cache_mem_k[l], cache_mem_v[l], state_sconv[l], state_ssm_conv[l],
            state_ssm[l].reshape(bs, SSD_GROUPS, SSD_HPG, SSD_HEAD_DIM, SSD_STATE),
            *lw)
        sc_s.append(sb)
        cc_s.append(cb)
        h_s.append(h.reshape(bs, SSD_HEADS, SSD_HEAD_DIM, SSD_STATE))
    y_prompt = rmsnorm(xp, norm_final)
    y_sample = rmsnorm(xs, norm_final)
    return (y_prompt, y_sample, jnp.stack(mk_p), jnp.stack(mv_p), jnp.stack(sc_p), jnp.stack(cc_p), jnp.stack(h_p),
            jnp.stack(sc_s), jnp.stack(cc_s), jnp.stack(h_s))
```

```python
import functools

import jax
import jax.numpy as jnp
from jax import lax
from jax.experimental import pallas as pl
from jax.experimental.pallas import tpu as pltpu

F32 = jnp.float32
BF16 = jnp.bfloat16
EPS = 1e-6
SSD_HEAD_DIM = 64
SSD_CHUNK = 128
N_EXPERT_GROUPS = 8
TOP_K = 2
MOE_TILE = 256
MIB = 1024 * 1024
VMEM_CAP = 60 * MIB


def _tile(n, pref, mult):
    best = None
    for d in range(mult, min(n, pref) + 1, mult):
        if n % d == 0:
            best = d
    return best if best is not None else n


def _params(sem, vmem_bytes):
    return pltpu.CompilerParams(dimension_semantics=sem,
                                vmem_limit_bytes=int(min(max(vmem_bytes, 16 * MIB), VMEM_CAP)))


def _silu(x):
    return x * jax.nn.sigmoid(x)


def _softplus(x):
    return jnp.maximum(x, 0.0) + jnp.log1p(jnp.exp(-jnp.abs(x)))


def _rms_kernel(x_ref, g_ref, o_ref):
    x = x_ref[...]
    y = x * lax.rsqrt(jnp.mean(x * x, axis=-1, keepdims=True) + EPS)
    o_ref[...] = (y * g_ref[...]).astype(o_ref.dtype)


def rmsnorm(x, g, out_dtype):
    t, d = x.shape
    tr = _tile(t, 512, 16)
    return pl.pallas_call(
        _rms_kernel,
        grid=(t // tr,),
        in_specs=[pl.BlockSpec((tr, d), lambda i: (i, 0)),
                  pl.BlockSpec((1, d), lambda i: (0, 0))],
        out_specs=pl.BlockSpec((tr, d), lambda i: (i, 0)),
        out_shape=jax.ShapeDtypeStruct((t, d), out_dtype),
        compiler_params=_params(("arbitrary",), 6 * tr * d * 4),
        name="rmsnorm",
    )(x, g.reshape(1, d))


def _mm_kernel(a_ref, w_ref, *rest, n_extra, epilogue):
    extra = rest[:n_extra]
    o_ref = rest[n_extra]
    wbf_ref = rest[n_extra + 1]

    @pl.when(pl.program_id(1) == 0)
    def _():
        wbf_ref[...] = w_ref[...].reshape(wbf_ref.shape).astype(BF16)

    acc = jnp.dot(a_ref[...], wbf_ref[...], preferred_element_type=F32)
    if epilogue is not None:
        acc = epilogue(acc, *[e[...] for e in extra])
    o_ref[...] = acc.astype(o_ref.dtype)


def mm(a, w, lead, *, k, ncols, tn, tm, kb_w=0, kb_a=0, col0=0, extras=(), epilogue=None,
       out_dtype=F32, name="mm"):
    m = a.shape[0]
    assert m % tm == 0 and ncols % tn == 0
    nj, ni = ncols // tn, m // tm
    lead = tuple(lead)
    if col0 % tn == 0:
        jb0 = col0 // tn
        w_spec = pl.BlockSpec((None,) * len(lead) + (k, tn), lambda j, i: lead + (kb_w, jb0 + j))
    else:
        w_spec = pl.BlockSpec((pl.Element(1),) * len(lead) + (pl.Element(k), pl.Element(tn)),
                              lambda j, i: lead + (kb_w * k, pl.multiple_of(col0 + j * tn, 128)))
    in_specs = [pl.BlockSpec((tm, k), lambda j, i: (i, kb_a)), w_spec]
    for _, off in extras:
        in_specs.append(pl.BlockSpec((tm, tn), lambda j, i, off=off: (i, off + j)))
    out_bytes = jnp.dtype(out_dtype).itemsize
    vmem = (2 * k * tn * 4 + k * tn * 2 + 2 * tm * k * 2 + 2 * tm * tn * out_bytes
            + sum(2 * tm * tn * e.dtype.itemsize for e, _ in extras) + 3 * tm * tn * 4 + 2 * MIB)
    return pl.pallas_call(
        functools.partial(_mm_kernel, n_extra=len(extras), epilogue=epilogue),
        grid=(nj, ni),
        in_specs=in_specs,
        out_specs=pl.BlockSpec((tm, tn), lambda j, i: (i, j)),
        out_shape=jax.ShapeDtypeStruct((m, ncols), out_dtype),
        scratch_shapes=[pltpu.VMEM((k, tn), BF16)],
        compiler_params=_params(("arbitrary", "arbitrary"), vmem),
        name=name,
    )(a, w, *[e for e, _ in extras])


def _shift_rows(x, s):
    row = lax.broadcasted_iota(jnp.int32, x.shape, 0)
    return jnp.where(row >= s, pltpu.roll(x, s, 0), 0.0)


def _sconv_kernel(cb_ref, cc_ref, cx_ref, w_ref, s_ref, v_ref, stp_ref, un_ref, *, nb, seq, ts):
    b = pl.program_id(1)
    w = w_ref[...]

    @pl.when(b < nb)
    def _():
        u = cc_ref[...] * cx_ref[...]
        y = w[0:1] * _shift_rows(u, 2) + w[1:2] * _shift_rows(u, 1) + w[2:3] * u
        v_ref[...] = (cb_ref[...] * y).astype(v_ref.dtype)
        stp_ref[...] = u[seq - 2:seq, :]

    @pl.when(b == nb)
    def _():
        u = cc_ref[0:ts, :] * cx_ref[0:ts, :]
        y = w[0:1] * s_ref[0] + w[1:2] * s_ref[1] + w[2:3] * u
        v_ref[0:ts, :] = (cb_ref[0:ts, :] * y).astype(v_ref.dtype)
        un_ref[...] = u


def sconv(c3, w, state_t, *, nb, seq, ts):
    t = c3.shape[0]
    c = c3.shape[1] // 3
    tc = _tile(c, 256, 128)
    ncb = c // tc
    kern = functools.partial(_sconv_kernel, nb=nb, seq=seq, ts=ts)
    return pl.pallas_call(
        kern,
        grid=(ncb, nb + 1),
        in_specs=[pl.BlockSpec((seq, tc), lambda j, b: (b, j)),
                  pl.BlockSpec((seq, tc), lambda j, b: (b, ncb + j)),
                  pl.BlockSpec((seq, tc), lambda j, b: (b, 2 * ncb + j)),
                  pl.BlockSpec((3, tc), lambda j, b: (0, j)),
                  pl.BlockSpec((2, ts, tc), lambda j, b: (0, 0, j))],
        out_specs=[pl.BlockSpec((seq, tc), lambda j, b: (b, j)),
                   pl.BlockSpec((None, 2, tc), lambda j, b: (jnp.minimum(b, nb - 1), 0, j)),
                   pl.BlockSpec((ts, tc), lambda j, b: (0, j))],
        out_shape=[jax.ShapeDtypeStruct((t, c), BF16),
                   jax.ShapeDtypeStruct((nb, 2, c), F32),
                   jax.ShapeDtypeStruct((ts, c), F32)],
        compiler_params=_params(("arbitrary", "arbitrary"), 16 * seq * tc * 4),
        name="sconv",
    )(c3, c3, c3, w, state_t)


def _xconv_kernel(x_ref, w_ref, bias_ref, s_ref, o_ref, stp_ref, os_ref, *, nb, seq, ts):
    b = pl.program_id(1)
    w = w_ref[...]
    bias = bias_ref[...]

    @pl.when(b < nb)
    def _():
        x = x_ref[...]
        y = (w[0:1] * _shift_rows(x, 3) + w[1:2] * _shift_rows(x, 2) + w[2:3] * _shift_rows(x, 1)
             + w[3:4] * x + bias)
        o_ref[...] = _silu(y).astype(o_ref.dtype)
        stp_ref[...] = x[seq - 3:seq, :]

    @pl.when(b == nb)
    def _():
        x = x_ref[0:ts, :]
        y = w[0:1] * s_ref[0] + w[1:2] * s_ref[1] + w[2:3] * s_ref[2] + w[3:4] * x + bias
        act = _silu(y)
        o_ref[0:ts, :] = act.astype(o_ref.dtype)
        os_ref[...] = act


def xconv(xraw, w, bias, state_t, *, col0, ncols, out_dtype, nb, seq, ts):
    t, ch = xraw.shape
    tc = _tile(ncols, 256, 128)
    assert col0 % tc == 0
    jb0 = col0 // tc
    kern = functools.partial(_xconv_kernel, nb=nb, seq=seq, ts=ts)
    return pl.pallas_call(
        kern,
        grid=(ncols // tc, nb + 1),
        in_specs=[pl.BlockSpec((seq, tc), lambda j, b: (b, jb0 + j)),
                  pl.BlockSpec((4, tc), lambda j, b: (0, jb0 + j)),
                  pl.BlockSpec((1, tc), lambda j, b: (0, jb0 + j)),
                  pl.BlockSpec((3, ts, tc), lambda j, b: (0, 0, jb0 + j))],
        out_specs=[pl.BlockSpec((seq, tc), lambda j, b: (b, j)),
                   pl.BlockSpec((None, 3, tc), lambda j, b: (jnp.minimum(b, nb - 1), 0, j)),
                   pl.BlockSpec((ts, tc), lambda j, b: (0, j))],
        out_shape=[jax.ShapeDtypeStruct((t, ncols), out_dtype),
                   jax.ShapeDtypeStruct((nb, 3, ncols), F32),
                   jax.ShapeDtypeStruct((ts, ncols), F32)],
        compiler_params=_params(("arbitrary", "arbitrary"), 16 * seq * tc * 4),
        name="xconv",
    )(xraw, w, bias.reshape(1, ch), state_t)


def _cumsum_rows(x):
    n = x.shape[0]
    row = lax.broadcasted_iota(jnp.int32, x.shape, 0)
    s = 1
    while s < n:
        x = x + jnp.where(row >= s, pltpu.roll(x, s, 0), 0.0)
        s *= 2
    return x


def _expand_heads(v, onehot):
    v1 = v.astype(BF16)
    r1 = v - v1.astype(F32)
    v2 = r1.astype(BF16)
    v3 = (r1 - v2.astype(F32)).astype(BF16)
    d = functools.partial(jnp.dot, preferred_element_type=F32)
    return d(v1, onehot) + d(v2, onehot) + d(v3, onehot)


def _ssd_prompt_kernel(xs_ref, bm_ref, cm_ref, dt_ref, z_ref, nrm_ref, dtb_ref, alog_ref, dsk_ref,
                       oh_ref, yn_ref, hfin_ref, st_ref, act_ref, *, hpg, nchunk):
    g = pl.program_id(1)
    c = pl.program_id(2)
    q = xs_ref.shape[0]
    p = SSD_HEAD_DIM

    @pl.when(c == 0)
    def _():
        st_ref[...] = jnp.zeros_like(st_ref)

    dt_h = _softplus(dt_ref[...] + dtb_ref[...])
    a_h = -jnp.exp(alog_ref[...])
    acum_h = _cumsum_rows(dt_h * a_h)
    act_ref[...] = acum_h.T
    onehot = oh_ref[...]
    dt_c = _expand_heads(dt_h, onehot)
    acum = _expand_heads(acum_h, onehot)

    xs = xs_ref[...]
    bm = bm_ref[...]
    cm = cm_ref[...]
    xdt = (xs * dt_c).astype(BF16)
    cb = lax.dot_general(cm, bm, (((1,), (1,)), ((), ())), preferred_element_type=F32)
    li = lax.broadcasted_iota(jnp.int32, (q, q), 0)
    si = lax.broadcasted_iota(jnp.int32, (q, q), 1)
    causal = li >= si
    first_half = si < p

    parts = []
    for pr in range(hpg // 2):
        rhs = xdt[:, pr * 2 * p:(pr + 1) * 2 * p]
        res = []
        for kk in range(2):
            hl = 2 * pr + kk
            col = acum[:, hl * p:hl * p + 1]
            row = act_ref[pl.ds(g * hpg + hl, 1), :]
            dec = jnp.exp(jnp.where(causal, col - row, -jnp.inf))
            res.append(jnp.dot((cb * dec).astype(BF16), rhs, preferred_element_type=F32))
        parts.append(jnp.where(first_half, res[0], res[1]))
    y = jnp.concatenate(parts, axis=1)

    st = st_ref[...]
    y = y + jnp.dot(cm, st.astype(BF16), preferred_element_type=F32) * jnp.exp(acum)
    alast = acum[q - 1:q, :]
    xw = (xs * dt_c * jnp.exp(alast - acum)).astype(BF16)
    bm_t = bm.astype(F32).T.astype(BF16)
    st_new = jnp.exp(alast) * st + jnp.dot(bm_t, xw, preferred_element_type=F32)
    st_ref[...] = st_new

    @pl.when(c == nchunk - 1)
    def _():
        hfin_ref[...] = st_new.T.reshape(hfin_ref.shape)

    y = y + dsk_ref[...] * xs
    v = y * _silu(z_ref[...])
    v = v * lax.rsqrt(jnp.mean(v * v, axis=-1, keepdims=True) + EPS)
    yn_ref[...] = (v * nrm_ref[...]).astype(yn_ref.dtype)


def ssd_prompt(xs, bc, dtraw, z, nrm, dtb, alog, dsk_c, onehot, *, nb, seq, t, d_inner, groups, nstate):
    heads = d_inner // SSD_HEAD_DIM
    hpg = heads // groups
    gc = d_inner // groups
    q = SSD_CHUNK if seq % SSD_CHUNK == 0 else seq
    nchunk = seq // q
    assert nstate == 128 and q % 8 == 0 and gc % 128 == 0 and hpg % 2 == 0
    kern = functools.partial(_ssd_prompt_kernel, hpg=hpg, nchunk=nchunk)
    rb = lambda b, g, c: b * nchunk + c
    return pl.pallas_call(
        kern,
        grid=(nb, groups, nchunk),
        in_specs=[pl.BlockSpec((q, gc), lambda b, g, c: (rb(b, g, c), g)),
                  pl.BlockSpec((q, nstate), lambda b, g, c: (rb(b, g, c), g)),
                  pl.BlockSpec((q, nstate), lambda b, g, c: (rb(b, g, c), groups + g)),
                  pl.BlockSpec((q, heads), lambda b, g, c: (rb(b, g, c), 0)),
                  pl.BlockSpec((q, gc), lambda b, g, c: (rb(b, g, c), g)),
                  pl.BlockSpec((1, gc), lambda b, g, c: (0, g)),
                  pl.BlockSpec((1, heads), lambda b, g, c: (0, 0)),
                  pl.BlockSpec((1, heads), lambda b, g, c: (0, 0)),
                  pl.BlockSpec((1, gc), lambda b, g, c: (0, g)),
                  pl.BlockSpec((heads, gc), lambda b, g, c: (0, g))],
        out_specs=[pl.BlockSpec((q, gc), lambda b, g, c: (rb(b, g, c), g)),
                   pl.BlockSpec((None, hpg, SSD_HEAD_DIM, nstate), lambda b, g, c: (b, g, 0, 0))],
        out_shape=[jax.ShapeDtypeStruct((t, d_inner), BF16),
                   jax.ShapeDtypeStruct((nb, heads, SSD_HEAD_DIM, nstate), F32)],
        scratch_shapes=[pltpu.VMEM((nstate, gc), F32), pltpu.VMEM((heads, q), F32)],
        compiler_params=_params(("arbitrary", "arbitrary", "arbitrary"), 32 * MIB),
        name="ssd_prompt",
    )(xs, bc, bc, dtraw, z, nrm.reshape(1, d_inner), dtb.reshape(1, heads), alog.reshape(1, heads),
      dsk_c.reshape(1, d_inner), onehot)


def _ssd_sample_kernel(dtb_ref, alog_ref, dsk_ref, h0_ref, dtt_ref, xst_ref, bm_ref, bmt_ref, cmt_ref,
                       hn_ref, yt_ref, *, ts):
    k = pl.program_id(0)
    p = SSD_HEAD_DIM
    dt = _softplus(dtt_ref[pl.ds(k, 1), :] + dtb_ref[k])
    a = -jnp.exp(jnp.full((1, ts), alog_ref[k], F32))
    dec_t = jnp.broadcast_to(jnp.exp(dt * a), (p, ts))
    xs_t = xst_ref[...]
    xdt_t = xs_t * dt
    for j in range(ts):
        hn_ref[j] = h0_ref[j] * dec_t[:, j:j + 1] + xdt_t[:, j:j + 1] * bm_ref[j:j + 1, :]
    h0 = h0_ref[...]
    cm_t = cmt_ref[...]
    res = jnp.dot(h0.reshape(ts * p, h0.shape[2]).astype(BF16), cm_t.astype(BF16),
                  preferred_element_type=F32).reshape(ts, p, ts)
    jj = lax.broadcasted_iota(jnp.int32, res.shape, 0)
    ll = lax.broadcasted_iota(jnp.int32, res.shape, 2)
    y_off = jnp.sum(jnp.where(jj == ll, res, 0.0), axis=0)
    cb = jnp.sum(cm_t * bmt_ref[...], axis=0, keepdims=True)
    yt_ref[...] = cb * xdt_t + y_off * dec_t + dsk_ref[k] * xs_t


def ssd_sample(h0, dtraw_t, xs_t, bm, bm_t, cm_t, dtb, alog, dsk, *, hpg):
    ts, rows, nstate = h0.shape
    p = SSD_HEAD_DIM
    heads = rows // p
    smem = pl.BlockSpec(memory_space=pltpu.SMEM)
    return pl.pallas_call(
        functools.partial(_ssd_sample_kernel, ts=ts),
        grid=(heads,),
        in_specs=[smem, smem, smem,
                  pl.BlockSpec((ts, p, nstate), lambda k: (0, k, 0)),
                  pl.BlockSpec((heads, ts), lambda k: (0, 0)),
                  pl.BlockSpec((p, ts), lambda k: (k, 0)),
                  pl.BlockSpec((None, ts, nstate), lambda k: (k // hpg, 0, 0)),
                  pl.BlockSpec((None, nstate, ts), lambda k: (k // hpg, 0, 0)),
                  pl.BlockSpec((None, nstate, ts), lambda k: (k // hpg, 0, 0))],
        out_specs=[pl.BlockSpec((ts, p, nstate), lambda k: (0, k, 0)),
                   pl.BlockSpec((p, ts), lambda k: (k, 0))],
        out_shape=[jax.ShapeDtypeStruct((ts, rows, nstate), F32),
                   jax.ShapeDtypeStruct((rows, ts), F32)],
        compiler_params=_params(("arbitrary",), 40 * MIB),
        name="ssd_sample",
    )(dtb, alog, dsk, h0, dtraw_t, xs_t, bm, bm_t, cm_t)


def _gnorm_kernel(y_ref, z_ref, nrm_ref, yn_in_ref, o_ref):
    del yn_in_ref
    v = y_ref[...] * _silu(z_ref[...])
    v = v * lax.rsqrt(jnp.mean(v * v, axis=-1, keepdims=True) + EPS)
    o_ref[...] = (v * nrm_ref[...]).astype(o_ref.dtype)


def gated_norm_sample(y_s, z, nrm, yn, *, tp, groups):
    ts, d_inner = y_s.shape
    gc = d_inner // groups
    assert tp % ts == 0
    rb = tp // ts
    return pl.pallas_call(
        _gnorm_kernel,
        grid=(groups,),
        in_specs=[pl.BlockSpec((ts, gc), lambda g: (0, g)),
                  pl.BlockSpec((ts, gc), lambda g: (rb, g)),
                  pl.BlockSpec((1, gc), lambda g: (0, g)),
                  pl.BlockSpec(memory_space=pl.ANY)],
        out_specs=pl.BlockSpec((ts, gc), lambda g: (rb, g)),
        out_shape=jax.ShapeDtypeStruct(yn.shape, yn.dtype),
        input_output_aliases={3: 0},
        compiler_params=_params(("arbitrary",), 16 * MIB),
        name="gated_norm_sample",
    )(y_s, z, nrm.reshape(1, d_inner), yn)


def _attn_kernel(q_ref, k_ref, v_ref, o_ref, *, nh, hd):
    q = q_ref[...]
    k = k_ref[...].astype(BF16)
    v = v_ref[...].astype(BF16)
    scale = hd ** -0.5
    for h in range(nh):
        sl = slice(h * hd, (h + 1) * hd)
        s = lax.dot_general(q[:, sl], k[:, sl], (((1,), (1,)), ((), ())),
                            preferred_element_type=F32) * scale
        e = jnp.exp(s - jnp.max(s, axis=-1, keepdims=True))
        pr = e / jnp.sum(e, axis=-1, keepdims=True)
        o = jnp.dot(pr.astype(BF16), v[:, sl], preferred_element_type=F32)
        o_ref[:, sl] = o.astype(o_ref.dtype)


def attention(q, k, v, *, nh, hd):
    b, lq, w = q.shape
    m = k.shape[1]
    tq = _tile(lq, 512, 8)
    return pl.pallas_call(
        functools.partial(_attn_kernel, nh=nh, hd=hd),
        grid=(b, lq // tq),
        in_specs=[pl.BlockSpec((None, tq, w), lambda bi, i: (bi, i, 0)),
                  pl.BlockSpec((None, m, w), lambda bi, i: (bi, 0, 0)),
                  pl.BlockSpec((None, m, w), lambda bi, i: (bi, 0, 0))],
        out_specs=pl.BlockSpec((None, tq, w), lambda bi, i: (bi, i, 0)),
        out_shape=jax.ShapeDtypeStruct((b, lq, w), BF16),
        compiler_params=_params(("arbitrary", "arbitrary"), 24 * MIB),
        name="attention",
    )(q, k, v)


def _router_kernel(x_ref, g_ref, w_ref, xn_ref, lg_ref):
    x = x_ref[...]
    xn = (x * lax.rsqrt(jnp.mean(x * x, axis=-1, keepdims=True) + EPS) * g_ref[...]).astype(BF16)
    xn_ref[...] = xn
    lg_ref[...] = jnp.dot(xn, w_ref[...], preferred_element_type=F32)


def router(x, g, w_router):
    t, d = x.shape
    e = w_router.shape[1]
    epad = -(-e // 128) * 128
    w = jnp.pad(w_router, ((0, 0), (0, epad - e))).astype(BF16)
    tr = _tile(t, 512, 16)
    return pl.pallas_call(
        _router_kernel,
        grid=(t // tr,),
        in_specs=[pl.BlockSpec((tr, d), lambda i: (i, 0)),
                  pl.BlockSpec((1, d), lambda i: (0, 0)),
                  pl.BlockSpec((d, epad), lambda i: (0, 0))],
        out_specs=[pl.BlockSpec((tr, d), lambda i: (i, 0)),
                   pl.BlockSpec((tr, epad), lambda i: (i, 0))],
        out_shape=[jax.ShapeDtypeStruct((t, d), BF16),
                   jax.ShapeDtypeStruct((t, epad), F32)],
        compiler_params=_params(("arbitrary",), 8 * tr * d * 4 + 8 * MIB),
        name="router",
    )(x, g.reshape(1, d), w)


def _moe_up_kernel(te_ref, tf_ref, nt_ref, x_ref, rw_ref, wg_ref, wu_ref, h_ref, wgb_ref, wub_ref):
    t = pl.program_id(1)

    @pl.when(t < nt_ref[0])
    def _():
        @pl.when(tf_ref[t] == 1)
        def _():
            wgb_ref[...] = wg_ref[...].astype(BF16)
            wub_ref[...] = wu_ref[...].astype(BF16)

        x = x_ref[...]
        gate = jnp.dot(x, wgb_ref[...], preferred_element_type=F32)
        up = jnp.dot(x, wub_ref[...], preferred_element_type=F32)
        h_ref[...] = (_silu(gate) * up * rw_ref[...]).astype(h_ref.dtype)


def moe_up(x_sorted, row_w, w_gate, w_up, layer, tile_e, tile_first, n_tiles):
    pr, d = x_sorted.shape
    f = w_gate.shape[-1]
    tf = _tile(f, 512, 128)
    ntile = pr // MOE_TILE

    def tix(t, nt):
        return jnp.minimum(t, nt[0] - 1)

    grid_spec = pltpu.PrefetchScalarGridSpec(
        num_scalar_prefetch=3,
        grid=(f // tf, ntile),
        in_specs=[pl.BlockSpec((MOE_TILE, d), lambda fb, t, te, tfi, nt: (tix(t, nt), 0)),
                  pl.BlockSpec((MOE_TILE, 1), lambda fb, t, te, tfi, nt: (tix(t, nt), 0)),
                  pl.BlockSpec((None, None, d, tf), lambda fb, t, te, tfi, nt: (layer, te[tix(t, nt)], 0, fb)),
                  pl.BlockSpec((None, None, d, tf), lambda fb, t, te, tfi, nt: (layer, te[tix(t, nt)], 0, fb))],
        out_specs=pl.BlockSpec((MOE_TILE, tf), lambda fb, t, te, tfi, nt: (tix(t, nt), fb)),
        scratch_shapes=[pltpu.VMEM((d, tf), BF16), pltpu.VMEM((d, tf), BF16)],
    )
    vmem = 4 * d * tf * 4 + 2 * d * tf * 2 + 2 * MOE_TILE * d * 2 + 8 * MOE_TILE * tf * 4 + 2 * MIB
    return pl.pallas_call(
        _moe_up_kernel,
        grid_spec=grid_spec,
        out_shape=jax.ShapeDtypeStruct((pr, f), BF16),
        compiler_params=_params(("arbitrary", "arbitrary"), vmem),
        name="moe_up",
    )(tile_e, tile_first, n_tiles, x_sorted, row_w, w_gate, w_up)


def _moe_down_kernel(te_ref, tf_ref, nt_ref, h_ref, wd_ref, o_ref, wdb_ref):
    t = pl.program_id(1)

    @pl.when(t < nt_ref[0])
    def _():
        @pl.when(tf_ref[t] == 1)
        def _():
            wdb_ref[...] = wd_ref[...].astype(BF16)

        o_ref[...] = jnp.dot(h_ref[...], wdb_ref[...], preferred_element_type=F32)


def moe_down(h_sorted, w_down, layer, tile_e, tile_first, n_tiles):
    pr, f = h_sorted.shape
    d = w_down.shape[-1]
    tn = _tile(d, 1024, 128)
    ntile = pr // MOE_TILE

    def tix(t, nt):
        return jnp.minimum(t, nt[0] - 1)

    grid_spec = pltpu.PrefetchScalarGridSpec(
        num_scalar_prefetch=3,
        grid=(d // tn, ntile),
        in_specs=[pl.BlockSpec((MOE_TILE, f), lambda nb, t, te, tfi, nt: (tix(t, nt), 0)),
                  pl.BlockSpec((None, None, f, tn), lambda nb, t, te, tfi, nt: (layer, te[tix(t, nt)], 0, nb))],
        out_specs=pl.BlockSpec((MOE_TILE, tn), lambda nb, t, te, tfi, nt: (tix(t, nt), nb)),
        scratch_shapes=[pltpu.VMEM((f, tn), BF16)],
    )
    vmem = 2 * f * tn * 4 + f * tn * 2 + 2 * MOE_TILE * f * 2 + 4 * MOE_TILE * tn * 4 + 2 * MIB
    return pl.pallas_call(
        _moe_down_kernel,
        grid_spec=grid_spec,
        out_shape=jax.ShapeDtypeStruct((pr, d), F32),
        compiler_params=_params(("arbitrary", "arbitrary"), vmem),
        name="moe_down",
    )(tile_e, tile_first, n_tiles, h_sorted, w_down)


def _route(logits, router_bias, n_experts):
    scores = jax.nn.sigmoid(logits[:, :n_experts])
    biased = scores + router_bias.astype(F32)
    grouped = biased.reshape(biased.shape[0], N_EXPERT_GROUPS, n_experts // N_EXPERT_GROUPS)
    group_score = jnp.sum(lax.top_k(grouped, 2)[0], axis=-1)
    top_group = jnp.argmax(group_score, axis=-1)
    in_group = top_group[:, None] == jnp.arange(N_EXPERT_GROUPS)
    masked = jnp.where(in_group[..., None], grouped, -jnp.inf).reshape(biased.shape)
    _, idx = lax.top_k(masked, TOP_K)
    sel = jnp.take_along_axis(scores, idx, axis=-1)
    wts = sel / jnp.sum(sel, axis=-1, keepdims=True)
    return idx, wts


def moe(x, xn, logits, router_bias, w_gate, w_up, w_down, layer):
    t, d = x.shape
    n_experts = w_gate.shape[1]
    idx, wts = _route(logits, router_bias, n_experts)
    na = t * TOP_K
    ntile = -(-na // MOE_TILE) + n_experts
    e_flat = idx.reshape(-1).astype(jnp.int32)
    tok = jnp.arange(na, dtype=jnp.int32) // TOP_K
    order = jnp.argsort(e_flat, stable=True)
    e_sorted = e_flat[order]
    counts = jnp.zeros((n_experts,), jnp.int32).at[e_flat].add(1)
    tiles_per = (counts + MOE_TILE - 1) // MOE_TILE
    tile_end = jnp.cumsum(tiles_per)
    tile_start = tile_end - tiles_per
    group_start = jnp.cumsum(counts) - counts
    pos_sorted = tile_start[e_sorted] * MOE_TILE + (jnp.arange(na, dtype=jnp.int32) - group_start[e_sorted])
    row_token = jnp.zeros((ntile * MOE_TILE,), jnp.int32).at[pos_sorted].set(tok[order])
    pos = jnp.zeros((na,), jnp.int32).at[order].set(pos_sorted).reshape(t, TOP_K)
    n_tiles = tile_end[-1:].astype(jnp.int32)
    tile_ids = jnp.arange(ntile, dtype=jnp.int32)
    tile_e = jnp.minimum(jnp.searchsorted(tile_end, tile_ids, side="right"), n_experts - 1).astype(jnp.int32)
    tile_first = jnp.concatenate([jnp.ones((1,), jnp.int32),
                                  (tile_e[1:] != tile_e[:-1]).astype(jnp.int32)])

    row_w = jnp.zeros((ntile * MOE_TILE, 1), F32).at[pos.reshape(-1), 0].set(wts.reshape(-1))

    x_sorted = jnp.take(xn, row_token, axis=0)
    h_sorted = moe_up(x_sorted, row_w, w_gate, w_up, layer, tile_e, tile_first, n_tiles)
    o_sorted = moe_down(h_sorted, w_down, layer, tile_e, tile_first, n_tiles)
    y = jnp.take(o_sorted, pos[:, 0], axis=0)
    for kk in range(1, TOP_K):
        y = y + jnp.take(o_sorted, pos[:, kk], axis=0)
    return x + y


def kernel(x_prompt, x_sample, mem_prompt, cache_mem_k, cache_mem_v, state_sconv, state_ssm_conv, state_ssm,
           norm_mix, w_in, sconv_w, w_sconv_out, ssd_conv_w, ssd_conv_b, ssd_dt_bias, ssd_a_log, ssd_d, ssd_norm,
           w_ssd_out, w_mix_out, norm_xattn, norm_mem, w_xk, w_xv, w_xq, w_xo, norm_moe, w_router, router_bias,
           w_gate, w_up, w_down, norm_final):
    nb, seq, d = x_prompt.shape
    ts = x_sample.shape[0]
    depth = w_in.shape[0]
    tp = nb * seq
    t = tp + ts
    c = sconv_w.shape[-1]
    d_inner = ssd_norm.shape[-1]
    heads = ssd_dt_bias.shape[-1]
    nstate = state_ssm.shape[-1]
    ch = ssd_conv_w.shape[-1]
    groups = (ch - d_inner) // (2 * nstate)
    hpg = heads // groups
    n_mem, nh, hd = cache_mem_k.shape[2:]
    xw = nh * hd
    assert x_sample.shape[1] == 1 and c == d and d_inner == 2 * d and heads * SSD_HEAD_DIM == d_inner
    assert ts <= seq and tp % ts == 0 and heads % 128 == 0

    col_z = 3 * c
    col_xbc = col_z + d_inner
    col_dt = col_xbc + ch
    col_g = col_dt + heads

    tm = _tile(t, 1040, 16)
    tn = 512
    mmk = functools.partial(mm, k=d, tm=tm)
    sig = lambda acc: jax.nn.sigmoid(acc)

    onehot = (jnp.arange(heads)[:, None] == (jnp.arange(d_inner) // SSD_HEAD_DIM)[None, :]).astype(BF16)
    mem2 = mem_prompt.reshape(nb * n_mem, d)

    x = jnp.concatenate([x_prompt.reshape(tp, d), x_sample.reshape(ts, d)], axis=0)
    outs = {n: [] for n in ("mk", "mv", "scp", "ccp", "hp", "scs", "ccs", "hs")}

    for l in range(depth):
        xn = rmsnorm(x, norm_mix[l], BF16)
        c3 = mmk(xn, w_in, (l,), col0=0, ncols=3 * c, tn=tn, name="in_conv")
        z = mmk(xn, w_in, (l,), col0=col_z, ncols=d_inner, tn=tn, name="in_z")
        xraw = mmk(xn, w_in, (l,), col0=col_xbc, ncols=ch, tn=tn, name="in_xbc")
        dtraw = mmk(xn, w_in, (l,), col0=col_dt, ncols=heads, tn=heads, name="in_dt")
        gates = mmk(xn, w_in, (l,), col0=col_g, ncols=2 * d, tn=tn, epilogue=sig, name="in_gates")

        sc_state = jnp.moveaxis(state_sconv[l], 1, 0)
        v, scp, u_new = sconv(c3, sconv_w[l], sc_state, nb=nb, seq=seq, ts=ts)
        cc_state = jnp.moveaxis(state_ssm_conv[l], 1, 0)
        cargs = dict(nb=nb, seq=seq, ts=ts)
        xs, ccp_x, xs_s = xconv(xraw, ssd_conv_w[l], ssd_conv_b[l], cc_state, col0=0, ncols=d_inner,
                                out_dtype=F32, **cargs)
        bc, ccp_bc, bc_s = xconv(xraw, ssd_conv_w[l], ssd_conv_b[l], cc_state, col0=d_inner,
                                 ncols=ch - d_inner, out_dtype=BF16, **cargs)
        ccp = jnp.concatenate([ccp_x, ccp_bc], axis=-1)

        dsk_c = jnp.repeat(ssd_d[l], SSD_HEAD_DIM)
        yn, hp = ssd_prompt(xs, bc, dtraw, z, ssd_norm[l], ssd_dt_bias[l], ssd_a_log[l], dsk_c, onehot,
                            nb=nb, seq=seq, t=t, d_inner=d_inner, groups=groups, nstate=nstate)

        bm_s = bc_s[:, :groups * nstate].reshape(ts, groups, nstate)
        cm_s = bc_s[:, groups * nstate:].reshape(ts, groups, nstate)
        hs, y_t = ssd_sample(state_ssm[l].reshape(ts, d_inner, nstate), dtraw[tp:].T, xs_s.T,
                             jnp.moveaxis(bm_s, 1, 0), jnp.transpose(bm_s, (1, 2, 0)),
                             jnp.transpose(cm_s, (1, 2, 0)),
                             ssd_dt_bias[l], ssd_a_log[l], ssd_d[l], hpg=hpg)
        yn = gated_norm_sample(y_t.T, z, ssd_norm[l], yn, tp=tp, groups=groups)

        y_conv = mmk(v, w_sconv_out, (l,), ncols=d, tn=tn, extras=[(gates, 0)],
                     epilogue=lambda acc, gc_: acc * gc_, name="sconv_out")
        y_ssd0 = mmk(yn, w_ssd_out, (l,), ncols=d, tn=tn, kb_a=0, kb_w=0, name="ssd_out0")
        mixed = mmk(yn, w_ssd_out, (l,), ncols=d, tn=tn, kb_a=1, kb_w=1,
                    extras=[(y_ssd0, 0), (y_conv, 0), (gates, d // tn)],
                    epilogue=lambda acc, y0, yc, gs: yc + gs * (y0 + acc), out_dtype=BF16, name="ssd_out1")
        x = mmk(mixed, w_mix_out, (l,), ncols=d, tn=tn, extras=[(x, 0)],
                epilogue=lambda acc, xr: xr + acc, name="mix_out")

        memn = rmsnorm(mem2, norm_mem[l], BF16)
        tmm = _tile(nb * n_mem, 1024, 16)
        k_p = mm(memn, w_xk, (l,), k=d, ncols=xw, tn=tn, tm=tmm, name="mem_k")
        v_p = mm(memn, w_xv, (l,), k=d, ncols=xw, tn=tn, tm=tmm, name="mem_v")
        xn = rmsnorm(x, norm_xattn[l], BF16)
        qa = mmk(xn, w_xq, (l,), ncols=xw, tn=tn, out_dtype=BF16, name="xq")
        o_p = attention(qa[:tp].reshape(nb, seq, xw), k_p.reshape(nb, n_mem, xw), v_p.reshape(nb, n_mem, xw),
                        nh=nh, hd=hd)
        q_s = jnp.broadcast_to(qa[tp:].reshape(ts, 1, xw), (ts, 8, xw))
        o_s = attention(q_s, cache_mem_k[l].reshape(ts, n_mem, xw), cache_mem_v[l].reshape(ts, n_mem, xw),
                        nh=nh, hd=hd)
        o = jnp.concatenate([o_p.reshape(tp, xw), o_s[:, 0]], axis=0)
        x = mm(o, w_xo, (l,), k=xw, ncols=d, tn=1024, tm=tm, extras=[(x, 0)],
               epilogue=lambda acc, xr: xr + acc, name="xo")

        xn, logits = router(x, norm_moe[l], w_router)
        x = moe(x, xn, logits, router_bias, w_gate, w_up, w_down, l)

        outs["mk"].append(k_p.reshape(nb, n_mem, nh, hd))
        outs["mv"].append(v_p.reshape(nb, n_mem, nh, hd))
        outs["scp"].append(scp)
        outs["ccp"].append(ccp)
        outs["hp"].append(hp)
        outs["scs"].append(jnp.stack([state_sconv[l][:, 1], u_new], axis=1))
        outs["ccs"].append(jnp.concatenate([state_ssm_conv[l][:, 1:], xraw[tp:, None, :]], axis=1))
        outs["hs"].append(hs.reshape(ts, heads, SSD_HEAD_DIM, nstate))

    y = rmsnorm(x, norm_final, F32)
    st = lambda n: jnp.stack(outs[n])
    return (y[:tp].reshape(nb, seq, d), y[tp:].reshape(ts, 1, d), st("mk"), st("mv"), st("scp"), st("ccp"),
            st("hp"), st("scs"), st("ccs"), st("hs"))
```

```python
import functools

import jax
import jax.numpy as jnp
from jax import lax
from jax.experimental import pallas as pl
from jax.experimental.pallas import tpu as pltpu

F32 = jnp.float32
BF16 = jnp.bfloat16
EPS = 1e-6
SSD_HEAD_DIM = 64
SSD_CHUNK = 128
N_EXPERT_GROUPS = 8
TOP_K = 2
MOE_TILE = 256
MIB = 1024 * 1024
VMEM_CAP = 60 * MIB


def _tile(n, pref, mult):
    best = None
    for d in range(mult, min(n, pref) + 1, mult):
        if n % d == 0:
            best = d
    return best if best is not None else n


def _params(sem, vmem_bytes):
    return pltpu.CompilerParams(dimension_semantics=sem,
                                vmem_limit_bytes=int(min(max(vmem_bytes, 16 * MIB), VMEM_CAP)))


def _silu(x):
    return x * jax.nn.sigmoid(x)


def _softplus(x):
    return jnp.maximum(x, 0.0) + jnp.log1p(jnp.exp(-jnp.abs(x)))


def _rms_kernel(x_ref, g_ref, o_ref):
    x = x_ref[...]
    y = x * lax.rsqrt(jnp.mean(x * x, axis=-1, keepdims=True) + EPS)
    o_ref[...] = (y * g_ref[...]).astype(o_ref.dtype)


def rmsnorm(x, g, out_dtype):
    t, d = x.shape
    tr = _tile(t, 512, 16)
    return pl.pallas_call(
        _rms_kernel,
        grid=(t // tr,),
        in_specs=[pl.BlockSpec((tr, d), lambda i: (i, 0)),
                  pl.BlockSpec((1, d), lambda i: (0, 0))],
        out_specs=pl.BlockSpec((tr, d), lambda i: (i, 0)),
        out_shape=jax.ShapeDtypeStruct((t, d), out_dtype),
        compiler_params=_params(("arbitrary",), 6 * tr * d * 4),
        name="rmsnorm",
    )(x, g.reshape(1, d))


def _cast_dot(a_ref, w_ref, wbf_ref, chunks):
    k = wbf_ref.shape[0]
    ck = k // chunks
    lead = (0,) * (len(w_ref.shape) - 2)
    acc = None
    for kc in range(chunks):
        sl = slice(kc * ck, (kc + 1) * ck)
        wbf_ref[sl, :] = w_ref[lead + (sl, slice(None))].astype(BF16)
        part = jnp.dot(a_ref[:, sl], wbf_ref[sl, :], preferred_element_type=F32)
        acc = part if acc is None else acc + part
    return acc


def _mm_kernel(a_ref, w_ref, *rest, n_extra, epilogue, chunks):
    extra = rest[:n_extra]
    o_ref = rest[n_extra]
    wbf_ref = rest[n_extra + 1]

    def finish(acc):
        if epilogue is not None:
            acc = epilogue(acc, *[e[...] for e in extra])
        o_ref[...] = acc.astype(o_ref.dtype)

    @pl.when(pl.program_id(1) == 0)
    def _():
        finish(_cast_dot(a_ref, w_ref, wbf_ref, chunks))

    @pl.when(pl.program_id(1) != 0)
    def _():
        finish(jnp.dot(a_ref[...], wbf_ref[...], preferred_element_type=F32))


def mm(a, w, lead, *, k, ncols, tn, tm, kb_w=0, kb_a=0, col0=0, extras=(), epilogue=None,
       out_dtype=F32, name="mm"):
    m = a.shape[0]
    assert m % tm == 0 and ncols % tn == 0
    nj, ni = ncols // tn, m // tm
    lead = tuple(lead)
    if col0 % tn == 0:
        jb0 = col0 // tn
        w_spec = pl.BlockSpec((None,) * len(lead) + (k, tn), lambda j, i: lead + (kb_w, jb0 + j))
    else:
        w_spec = pl.BlockSpec((pl.Element(1),) * len(lead) + (pl.Element(k), pl.Element(tn)),
                              lambda j, i: lead + (kb_w * k, pl.multiple_of(col0 + j * tn, 128)))
    in_specs = [pl.BlockSpec((tm, k), lambda j, i: (i, kb_a)), w_spec]
    for _, off in extras:
        in_specs.append(pl.BlockSpec((tm, tn), lambda j, i, off=off: (i, off + j)))
    out_bytes = jnp.dtype(out_dtype).itemsize
    vmem = (2 * k * tn * 4 + k * tn * 2 + 2 * tm * k * 2 + 2 * tm * tn * out_bytes
            + sum(2 * tm * tn * e.dtype.itemsize for e, _ in extras) + 3 * tm * tn * 4 + 2 * MIB)
    return pl.pallas_call(
        functools.partial(_mm_kernel, n_extra=len(extras), epilogue=epilogue, chunks=max(1, k // 512)),
        grid=(nj, ni),
        in_specs=in_specs,
        out_specs=pl.BlockSpec((tm, tn), lambda j, i: (i, j)),
        out_shape=jax.ShapeDtypeStruct((m, ncols), out_dtype),
        scratch_shapes=[pltpu.VMEM((k, tn), BF16)],
        compiler_params=_params(("arbitrary", "arbitrary"), vmem),
        name=name,
    )(a, w, *[e for e, _ in extras])


def _shift_rows(x, s):
    row = lax.broadcasted_iota(jnp.int32, x.shape, 0)
    return jnp.where(row >= s, pltpu.roll(x, s, 0), 0.0)


def _sconv_kernel(cb_ref, cc_ref, cx_ref, w_ref, s_ref, v_ref, stp_ref, un_ref, *, nb, seq, ts):
    b = pl.program_id(1)
    w = w_ref[...]

    @pl.when(b < nb)
    def _():
        u = cc_ref[...] * cx_ref[...]
        y = w[0:1] * _shift_rows(u, 2) + w[1:2] * _shift_rows(u, 1) + w[2:3] * u
        v_ref[...] = (cb_ref[...] * y).astype(v_ref.dtype)
        stp_ref[...] = u[seq - 2:seq, :]

    @pl.when(b == nb)
    def _():
        u = cc_ref[0:ts, :] * cx_ref[0:ts, :]
        y = w[0:1] * s_ref[0] + w[1:2] * s_ref[1] + w[2:3] * u
        v_ref[0:ts, :] = (cb_ref[0:ts, :] * y).astype(v_ref.dtype)
        un_ref[...] = u


def sconv(c3, w, state_t, *, nb, seq, ts):
    t = c3.shape[0]
    c = c3.shape[1] // 3
    tc = _tile(c, 256, 128)
    ncb = c // tc
    kern = functools.partial(_sconv_kernel, nb=nb, seq=seq, ts=ts)
    return pl.pallas_call(
        kern,
        grid=(ncb, nb + 1),
        in_specs=[pl.BlockSpec((seq, tc), lambda j, b: (b, j)),
                  pl.BlockSpec((seq, tc), lambda j, b: (b, ncb + j)),
                  pl.BlockSpec((seq, tc), lambda j, b: (b, 2 * ncb + j)),
                  pl.BlockSpec((3, tc), lambda j, b: (0, j)),
                  pl.BlockSpec((2, ts, tc), lambda j, b: (0, 0, j))],
        out_specs=[pl.BlockSpec((seq, tc), lambda j, b: (b, j)),
                   pl.BlockSpec((None, 2, tc), lambda j, b: (jnp.minimum(b, nb - 1), 0, j)),
                   pl.BlockSpec((ts, tc), lambda j, b: (0, j))],
        out_shape=[jax.ShapeDtypeStruct((t, c), BF16),
                   jax.ShapeDtypeStruct((nb, 2, c), F32),
                   jax.ShapeDtypeStruct((ts, c), F32)],
        compiler_params=_params(("arbitrary", "arbitrary"), 16 * seq * tc * 4),
        name="sconv",
    )(c3, c3, c3, w, state_t)


def _xconv_kernel(x_ref, w_ref, bias_ref, s_ref, o_ref, stp_ref, os_ref, *, nb, seq, ts):
    b = pl.program_id(1)
    w = w_ref[...]
    bias = bias_ref[...]

    @pl.when(b < nb)
    def _():
        x = x_ref[...]
        y = (w[0:1] * _shift_rows(x, 3) + w[1:2] * _shift_rows(x, 2) + w[2:3] * _shift_rows(x, 1)
             + w[3:4] * x + bias)
        o_ref[...] = _silu(y).astype(o_ref.dtype)
        stp_ref[...] = x[seq - 3:seq, :]

    @pl.when(b == nb)
    def _():
        x = x_ref[0:ts, :]
        y = w[0:1] * s_ref[0] + w[1:2] * s_ref[1] + w[2:3] * s_ref[2] + w[3:4] * x + bias
        act = _silu(y)
        o_ref[0:ts, :] = act.astype(o_ref.dtype)
        os_ref[...] = act


def xconv(xraw, w, bias, state_t, *, col0, ncols, out_dtype, nb, seq, ts):
    t, ch = xraw.shape
    tc = _tile(ncols, 256, 128)
    assert col0 % tc == 0
    jb0 = col0 // tc
    kern = functools.partial(_xconv_kernel, nb=nb, seq=seq, ts=ts)
    return pl.pallas_call(
        kern,
        grid=(ncols // tc, nb + 1),
        in_specs=[pl.BlockSpec((seq, tc), lambda j, b: (b, jb0 + j)),
                  pl.BlockSpec((4, tc), lambda j, b: (0, jb0 + j)),
                  pl.BlockSpec((1, tc), lambda j, b: (0, jb0 + j)),
                  pl.BlockSpec((3, ts, tc), lambda j, b: (0, 0, jb0 + j))],
        out_specs=[pl.BlockSpec((seq, tc), lambda j, b: (b, j)),
                   pl.BlockSpec((None, 3, tc), lambda j, b: (jnp.minimum(b, nb - 1), 0, j)),
                   pl.BlockSpec((ts, tc), lambda j, b: (0, j))],
        out_shape=[jax.ShapeDtypeStruct((t, ncols), out_dtype),
                   jax.ShapeDtypeStruct((nb, 3, ncols), F32),
                   jax.ShapeDtypeStruct((ts, ncols), F32)],
        compiler_params=_params(("arbitrary", "arbitrary"), 16 * seq * tc * 4),
        name="xconv",
    )(xraw, w, bias.reshape(1, ch), state_t)


def _cumsum_rows(x):
    n = x.shape[0]
    row = lax.broadcasted_iota(jnp.int32, x.shape, 0)
    s = 1
    while s < n:
        x = x + jnp.where(row >= s, pltpu.roll(x, s, 0), 0.0)
        s *= 2
    return x


def _expand_heads(v, onehot):
    v1 = v.astype(BF16)
    r1 = v - v1.astype(F32)
    v2 = r1.astype(BF16)
    v3 = (r1 - v2.astype(F32)).astype(BF16)
    d = functools.partial(jnp.dot, preferred_element_type=F32)
    return d(v1, onehot) + d(v2, onehot) + d(v3, onehot)


def _ssd_prompt_kernel(xs_ref, bm_ref, cm_ref, dt_ref, z_ref, nrm_ref, dtb_ref, alog_ref, dsk_ref,
                       oh_ref, yn_ref, hfin_ref, st_ref, act_ref, *, hpg, nchunk):
    g = pl.program_id(1)
    c = pl.program_id(2)
    q = xs_ref.shape[0]
    p = SSD_HEAD_DIM

    @pl.when(c == 0)
    def _():
        st_ref[...] = jnp.zeros_like(st_ref)

    dt_h = _softplus(dt_ref[...] + dtb_ref[...])
    a_h = -jnp.exp(alog_ref[...])
    acum_h = _cumsum_rows(dt_h * a_h)
    act_ref[...] = acum_h.T
    onehot = oh_ref[...]
    dt_c = _expand_heads(dt_h, onehot)
    acum = _expand_heads(acum_h, onehot)

    xs = xs_ref[...]
    bm = bm_ref[...]
    cm = cm_ref[...]
    xdt = (xs * dt_c).astype(BF16)
    cb = lax.dot_general(cm, bm, (((1,), (1,)), ((), ())), preferred_element_type=F32)
    li = lax.broadcasted_iota(jnp.int32, (q, q), 0)
    si = lax.broadcasted_iota(jnp.int32, (q, q), 1)
    causal = li >= si
    first_half = si < p

    parts = []
    for pr in range(hpg // 2):
        rhs = xdt[:, pr * 2 * p:(pr + 1) * 2 * p]
        res = []
        for kk in range(2):
            hl = 2 * pr + kk
            col = acum[:, hl * p:hl * p + 1]
            row = act_ref[pl.ds(g * hpg + hl, 1), :]
            dec = jnp.exp(jnp.where(causal, col - row, -jnp.inf))
            res.append(jnp.dot((cb * dec).astype(BF16), rhs, preferred_element_type=F32))
        parts.append(jnp.where(first_half, res[0], res[1]))
    y = jnp.concatenate(parts, axis=1)

    st = st_ref[...]
    y = y + jnp.dot(cm, st.astype(BF16), preferred_element_type=F32) * jnp.exp(acum)
    alast = acum[q - 1:q, :]
    xw = (xs * dt_c * jnp.exp(alast - acum)).astype(BF16)
    bm_t = bm.astype(F32).T.astype(BF16)
    st_new = jnp.exp(alast) * st + jnp.dot(bm_t, xw, preferred_element_type=F32)
    st_ref[...] = st_new

    @pl.when(c == nchunk - 1)
    def _():
        hfin_ref[...] = st_new.T.reshape(hfin_ref.shape)

    y = y + dsk_ref[...] * xs
    v = y * _silu(z_ref[...])
    v = v * lax.rsqrt(jnp.mean(v * v, axis=-1, keepdims=True) + EPS)
    yn_ref[...] = (v * nrm_ref[...]).astype(yn_ref.dtype)


def ssd_prompt(xs, bc, dtraw, z, nrm, dtb, alog, dsk_c, onehot, *, nb, seq, t, d_inner, groups, nstate):
    heads = d_inner // SSD_HEAD_DIM
    hpg = heads // groups
    gc = d_inner // groups
    q = SSD_CHUNK if seq % SSD_CHUNK == 0 else seq
    nchunk = seq // q
    assert nstate == 128 and q % 8 == 0 and gc % 128 == 0 and hpg % 2 == 0
    kern = functools.partial(_ssd_prompt_kernel, hpg=hpg, nchunk=nchunk)
    rb = lambda b, g, c: b * nchunk + c
    return pl.pallas_call(
        kern,
        grid=(nb, groups, nchunk),
        in_specs=[pl.BlockSpec((q, gc), lambda b, g, c: (rb(b, g, c), g)),
                  pl.BlockSpec((q, nstate), lambda b, g, c: (rb(b, g, c), g)),
                  pl.BlockSpec((q, nstate), lambda b, g, c: (rb(b, g, c), groups + g)),
                  pl.BlockSpec((q, heads), lambda b, g, c: (rb(b, g, c), 0)),
                  pl.BlockSpec((q, gc), lambda b, g, c: (rb(b, g, c), g)),
                  pl.BlockSpec((1, gc), lambda b, g, c: (0, g)),
                  pl.BlockSpec((1, heads), lambda b, g, c: (0, 0)),
                  pl.BlockSpec((1, heads), lambda b, g, c: (0, 0)),
                  pl.BlockSpec((1, gc), lambda b, g, c: (0, g)),
                  pl.BlockSpec((heads, gc), lambda b, g, c: (0, g))],
        out_specs=[pl.BlockSpec((q, gc), lambda b, g, c: (rb(b, g, c), g)),
                   pl.BlockSpec((None, hpg, SSD_HEAD_DIM, nstate), lambda b, g, c: (b, g, 0, 0))],
        out_shape=[jax.ShapeDtypeStruct((t, d_inner), BF16),
                   jax.ShapeDtypeStruct((nb, heads, SSD_HEAD_DIM, nstate), F32)],
        scratch_shapes=[pltpu.VMEM((nstate, gc), F32), pltpu.VMEM((heads, q), F32)],
        compiler_params=_params(("arbitrary", "arbitrary", "arbitrary"), 32 * MIB),
        name="ssd_prompt",
    )(xs, bc, bc, dtraw, z, nrm.reshape(1, d_inner), dtb.reshape(1, heads), alog.reshape(1, heads),
      dsk_c.reshape(1, d_inner), onehot)


def _ssd_sample_kernel(dtb_ref, alog_ref, dsk_ref, h0_ref, dtt_ref, xst_ref, bm_ref, bmt_ref, cmt_ref,
                       *rest, ts):
    hn_ref, yt_ref = rest[-2:]
    k = pl.program_id(0)
    p = SSD_HEAD_DIM
    dt = _softplus(dtt_ref[pl.ds(k, 1), :] + dtb_ref[k])
    a = -jnp.exp(jnp.full((1, ts), alog_ref[k], F32))
    dec_t = jnp.broadcast_to(jnp.exp(dt * a), (p, ts))
    xs_t = xst_ref[...]
    xdt_t = xs_t * dt
    for j in range(ts):
        hn_ref[j] = h0_ref[j] * dec_t[:, j:j + 1] + xdt_t[:, j:j + 1] * bm_ref[j:j + 1, :]
    h0 = h0_ref[...]
    cm_t = cmt_ref[...]
    res = jnp.dot(h0.reshape(ts * p, h0.shape[2]).astype(BF16), cm_t.astype(BF16),
                  preferred_element_type=F32).reshape(ts, p, ts)
    jj = lax.broadcasted_iota(jnp.int32, res.shape, 0)
    ll = lax.broadcasted_iota(jnp.int32, res.shape, 2)
    y_off = jnp.sum(jnp.where(jj == ll, res, 0.0), axis=0)
    cb = jnp.sum(cm_t * bmt_ref[...], axis=0, keepdims=True)
    yt_ref[...] = cb * xdt_t + y_off * dec_t + dsk_ref[k] * xs_t


def ssd_sample(h0_all, layer, hn_prev, dtraw_t, xs_t, bm, bm_t, cm_t, dtb, alog, dsk, *, hpg):
    depth, ts, rows, nstate = h0_all.shape
    p = SSD_HEAD_DIM
    heads = rows // p
    smem = pl.BlockSpec(memory_space=pltpu.SMEM)
    in_specs = [smem, smem, smem,
                pl.BlockSpec((None, ts, p, nstate), lambda k: (layer, 0, k, 0)),
                pl.BlockSpec((heads, ts), lambda k: (0, 0)),
                pl.BlockSpec((p, ts), lambda k: (k, 0)),
                pl.BlockSpec((None, ts, nstate), lambda k: (k // hpg, 0, 0)),
                pl.BlockSpec((None, nstate, ts), lambda k: (k // hpg, 0, 0)),
                pl.BlockSpec((None, nstate, ts), lambda k: (k // hpg, 0, 0))]
    args = [dtb, alog, dsk, h0_all, dtraw_t, xs_t, bm, bm_t, cm_t]
    aliases = {}
    if hn_prev is not None:
        in_specs.append(pl.BlockSpec(memory_space=pl.ANY))
        args.append(hn_prev)
        aliases = {len(args) - 1: 0}
    return pl.pallas_call(
        functools.partial(_ssd_sample_kernel, ts=ts),
        grid=(heads,),
        in_specs=in_specs,
        out_specs=[pl.BlockSpec((None, ts, p, nstate), lambda k: (layer, 0, k, 0)),
                   pl.BlockSpec((p, ts), lambda k: (k, 0))],
        out_shape=[jax.ShapeDtypeStruct((depth, ts, rows, nstate), F32),
                   jax.ShapeDtypeStruct((rows, ts), F32)],
        input_output_aliases=aliases,
        compiler_params=_params(("arbitrary",), 40 * MIB),
        name="ssd_sample",
    )(*args)


def _gnorm_kernel(y_ref, z_ref, nrm_ref, yn_in_ref, o_ref):
    del yn_in_ref
    v = y_ref[...] * _silu(z_ref[...])
    v = v * lax.rsqrt(jnp.mean(v * v, axis=-1, keepdims=True) + EPS)
    o_ref[...] = (v * nrm_ref[...]).astype(o_ref.dtype)


def gated_norm_sample(y_s, z, nrm, yn, *, tp, groups):
    ts, d_inner = y_s.shape
    gc = d_inner // groups
    assert tp % ts == 0
    rb = tp // ts
    return pl.pallas_call(
        _gnorm_kernel,
        grid=(groups,),
        in_specs=[pl.BlockSpec((ts, gc), lambda g: (0, g)),
                  pl.BlockSpec((ts, gc), lambda g: (rb, g)),
                  pl.BlockSpec((1, gc), lambda g: (0, g)),
                  pl.BlockSpec(memory_space=pl.ANY)],
        out_specs=pl.BlockSpec((ts, gc), lambda g: (rb, g)),
        out_shape=jax.ShapeDtypeStruct(yn.shape, yn.dtype),
        input_output_aliases={3: 0},
        compiler_params=_params(("arbitrary",), 16 * MIB),
        name="gated_norm_sample",
    )(y_s, z, nrm.reshape(1, d_inner), yn)


def _attn_kernel(q_ref, k_ref, v_ref, o_ref, *, nh, hd):
    q = q_ref[...]
    k = k_ref[...].astype(BF16)
    v = v_ref[...].astype(BF16)
    scale = hd ** -0.5
    for h in range(nh):
        sl = slice(h * hd, (h + 1) * hd)
        s = lax.dot_general(q[:, sl], k[:, sl], (((1,), (1,)), ((), ())),
                            preferred_element_type=F32) * scale
        e = jnp.exp(s - jnp.max(s, axis=-1, keepdims=True))
        pr = e / jnp.sum(e, axis=-1, keepdims=True)
        o = jnp.dot(pr.astype(BF16), v[:, sl], preferred_element_type=F32)
        o_ref[:, sl] = o.astype(o_ref.dtype)


def attention(q, k, v, *, nh, hd):
    b, lq, w = q.shape
    m = k.shape[1]
    tq = _tile(lq, 512, 8)
    return pl.pallas_call(
        functools.partial(_attn_kernel, nh=nh, hd=hd),
        grid=(b, lq // tq),
        in_specs=[pl.BlockSpec((None, tq, w), lambda bi, i: (bi, i, 0)),
                  pl.BlockSpec((None, m, w), lambda bi, i: (bi, 0, 0)),
                  pl.BlockSpec((None, m, w), lambda bi, i: (bi, 0, 0))],
        out_specs=pl.BlockSpec((None, tq, w), lambda bi, i: (bi, i, 0)),
        out_shape=jax.ShapeDtypeStruct((b, lq, w), BF16),
        compiler_params=_params(("arbitrary", "arbitrary"), 24 * MIB),
        name="attention",
    )(q, k, v)


def _router_kernel(x_ref, g_ref, w_ref, xn_ref, lg_ref):
    x = x_ref[...]
    xn = (x * lax.rsqrt(jnp.mean(x * x, axis=-1, keepdims=True) + EPS) * g_ref[...]).astype(BF16)
    xn_ref[...] = xn
    lg_ref[...] = jnp.dot(xn, w_ref[...], preferred_element_type=F32)


def router(x, g, w_router):
    t, d = x.shape
    e = w_router.shape[1]
    epad = -(-e // 128) * 128
    w = jnp.pad(w_router, ((0, 0), (0, epad - e))).astype(BF16)
    tr = _tile(t, 512, 16)
    return pl.pallas_call(
        _router_kernel,
        grid=(t // tr,),
        in_specs=[pl.BlockSpec((tr, d), lambda i: (i, 0)),
                  pl.BlockSpec((1, d), lambda i: (0, 0)),
                  pl.BlockSpec((d, epad), lambda i: (0, 0))],
        out_specs=[pl.BlockSpec((tr, d), lambda i: (i, 0)),
                   pl.BlockSpec((tr, epad), lambda i: (i, 0))],
        out_shape=[jax.ShapeDtypeStruct((t, d), BF16),
                   jax.ShapeDtypeStruct((t, epad), F32)],
        compiler_params=_params(("arbitrary",), 8 * tr * d * 4 + 8 * MIB),
        name="router",
    )(x, g.reshape(1, d), w)


def _moe_up_kernel(te_ref, tf_ref, nt_ref, x_ref, rw_ref, wg_ref, wu_ref, h_ref, wgb_ref, wub_ref):
    t = pl.program_id(1)

    def finish(gate, up):
        h_ref[...] = (_silu(gate) * up * rw_ref[...]).astype(h_ref.dtype)

    active = t < nt_ref[0]
    first = tf_ref[t] == 1
    chunks = max(1, wgb_ref.shape[0] // 512)

    @pl.when(active & first)
    def _():
        finish(_cast_dot(x_ref, wg_ref, wgb_ref, chunks), _cast_dot(x_ref, wu_ref, wub_ref, chunks))

    @pl.when(active & jnp.logical_not(first))
    def _():
        x = x_ref[...]
        finish(jnp.dot(x, wgb_ref[...], preferred_element_type=F32),
               jnp.dot(x, wub_ref[...], preferred_element_type=F32))


def moe_up(x_sorted, row_w, w_gate, w_up, layer, tile_e, tile_first, n_tiles):
    pr, d = x_sorted.shape
    f = w_gate.shape[-1]
    tf = _tile(f, 512, 128)
    ntile = pr // MOE_TILE

    def tix(t, nt):
        return jnp.minimum(t, nt[0] - 1)

    grid_spec = pltpu.PrefetchScalarGridSpec(
        num_scalar_prefetch=3,
        grid=(f // tf, ntile),
        in_specs=[pl.BlockSpec((MOE_TILE, d), lambda fb, t, te, tfi, nt: (tix(t, nt), 0)),
                  pl.BlockSpec((MOE_TILE, 1), lambda fb, t, te, tfi, nt: (tix(t, nt), 0)),
                  pl.BlockSpec((None, None, d, tf), lambda fb, t, te, tfi, nt: (layer, te[tix(t, nt)], 0, fb)),
                  pl.BlockSpec((None, None, d, tf), lambda fb, t, te, tfi, nt: (layer, te[tix(t, nt)], 0, fb))],
        out_specs=pl.BlockSpec((MOE_TILE, tf), lambda fb, t, te, tfi, nt: (tix(t, nt), fb)),
        scratch_shapes=[pltpu.VMEM((d, tf), BF16), pltpu.VMEM((d, tf), BF16)],
    )
    vmem = 4 * d * tf * 4 + 2 * d * tf * 2 + 2 * MOE_TILE * d * 2 + 8 * MOE_TILE * tf * 4 + 2 * MIB
    return pl.pallas_call(
        _moe_up_kernel,
        grid_spec=grid_spec,
        out_shape=jax.ShapeDtypeStruct((pr, f), BF16),
        compiler_params=_params(("arbitrary", "arbitrary"), vmem),
        name="moe_up",
    )(tile_e, tile_first, n_tiles, x_sorted, row_w, w_gate, w_up)


def _moe_down_kernel(te_ref, tf_ref, nt_ref, h_ref, wd_ref, o_ref, wdb_ref):
    t = pl.program_id(1)

    active = t < nt_ref[0]
    first = tf_ref[t] == 1

    @pl.when(active & first)
    def _():
        o_ref[...] = _cast_dot(h_ref, wd_ref, wdb_ref, max(1, wdb_ref.shape[0] // 256))

    @pl.when(active & jnp.logical_not(first))
    def _():
        o_ref[...] = jnp.dot(h_ref[...], wdb_ref[...], preferred_element_type=F32)


def moe_down(h_sorted, w_down, layer, tile_e, tile_first, n_tiles):
    pr, f = h_sorted.shape
    d = w_down.shape[-1]
    tn = _tile(d, 2048, 128)
    ntile = pr // MOE_TILE

    def tix(t, nt):
        return jnp.minimum(t, nt[0] - 1)

    grid_spec = pltpu.PrefetchScalarGridSpec(
        num_scalar_prefetch=3,
        grid=(d // tn, ntile),
        in_specs=[pl.BlockSpec((MOE_TILE, f), lambda nb, t, te, tfi, nt: (tix(t, nt), 0)),
                  pl.BlockSpec((None, None, f, tn), lambda nb, t, te, tfi, nt: (layer, te[tix(t, nt)], 0, nb))],
        out_specs=pl.BlockSpec((MOE_TILE, tn), lambda nb, t, te, tfi, nt: (tix(t, nt), nb)),
        scratch_shapes=[pltpu.VMEM((f, tn), BF16)],
    )
    vmem = 2 * f * tn * 4 + f * tn * 2 + 2 * MOE_TILE * f * 2 + 4 * MOE_TILE * tn * 4 + 2 * MIB
    return pl.pallas_call(
        _moe_down_kernel,
        grid_spec=grid_spec,
        out_shape=jax.ShapeDtypeStruct((pr, d), F32),
        compiler_params=_params(("arbitrary", "arbitrary"), vmem),
        name="moe_down",
    )(tile_e, tile_first, n_tiles, h_sorted, w_down)


def _dispatch_kernel(e_ref, pos_ref, tend_ref, cnt_ref, run_ref, tstart_ref, *, rows):
    ph = pl.program_id(0)
    i = pl.program_id(1)
    lanes = cnt_ref.shape[1]
    onehot = (e_ref[...] == lax.broadcasted_iota(jnp.int32, (rows, lanes), 1)).astype(F32)
    colsum = jnp.sum(onehot, axis=0, keepdims=True)

    @pl.when((ph == 0) & (i == 0))
    def _():
        cnt_ref[...] = jnp.zeros_like(cnt_ref)
        run_ref[...] = jnp.zeros_like(run_ref)

    @pl.when(ph == 0)
    def _():
        cnt_ref[...] += colsum

    @pl.when((ph == 1) & (i == 0))
    def _():
        tiles = jnp.floor((cnt_ref[...] + (MOE_TILE - 1)) * (1.0 / MOE_TILE))
        r = lax.broadcasted_iota(jnp.int32, (lanes, lanes), 0)
        c = lax.broadcasted_iota(jnp.int32, (lanes, lanes), 1)
        before = (r < c).astype(BF16)
        start = jnp.dot(jnp.broadcast_to(tiles, (8, lanes)).astype(BF16), before,
                        preferred_element_type=F32)[0:1]
        tstart_ref[...] = start
        tend_ref[...] = (start + tiles).astype(jnp.int32)

    @pl.when(ph == 1)
    def _():
        r = lax.broadcasted_iota(jnp.int32, (rows, rows), 0)
        c = lax.broadcasted_iota(jnp.int32, (rows, rows), 1)
        earlier = (c < r).astype(BF16)
        rank = jnp.dot(earlier, onehot.astype(BF16), preferred_element_type=F32) + run_ref[...]
        row = tstart_ref[...] * MOE_TILE + rank
        pos_ref[...] = jnp.sum(onehot * row, axis=1, keepdims=True).astype(jnp.int32)
        run_ref[...] += colsum


def dispatch(e_col, n_experts):
    na = e_col.shape[0]
    lanes = -(-n_experts // 128) * 128
    rows = _tile(na, 640, 128)
    assert rows % 128 == 0 and na + n_experts * MOE_TILE < 2 ** 24 and na // MOE_TILE + n_experts <= 256
    nblk = na // rows
    return pl.pallas_call(
        functools.partial(_dispatch_kernel, rows=rows),
        grid=(2, nblk),
        in_specs=[pl.BlockSpec((rows, 1), lambda ph, i: (i, 0))],
        out_specs=[pl.BlockSpec((rows, 1), lambda ph, i: (ph * i, 0)),
                   pl.BlockSpec((1, lanes), lambda ph, i: (0, 0))],
        out_shape=[jax.ShapeDtypeStruct((na, 1), jnp.int32),
                   jax.ShapeDtypeStruct((1, lanes), jnp.int32)],
        scratch_shapes=[pltpu.VMEM((1, lanes), F32), pltpu.VMEM((1, lanes), F32),
                        pltpu.VMEM((1, lanes), F32)],
        compiler_params=_params(("arbitrary", "arbitrary"), 16 * MIB),
        name="dispatch",
    )(e_col)


def _route(logits, router_bias, n_experts):
    scores = jax.nn.sigmoid(logits[:, :n_experts])
    biased = scores + router_bias.astype(F32)
    grouped = biased.reshape(biased.shape[0], N_EXPERT_GROUPS, n_experts // N_EXPERT_GROUPS)
    group_score = jnp.sum(lax.top_k(grouped, 2)[0], axis=-1)
    top_group = jnp.argmax(group_score, axis=-1)
    in_group = top_group[:, None] == jnp.arange(N_EXPERT_GROUPS)
    masked = jnp.where(in_group[..., None], grouped, -jnp.inf).reshape(biased.shape)
    _, idx = lax.top_k(masked, TOP_K)
    sel = jnp.take_along_axis(scores, idx, axis=-1)
    wts = sel / jnp.sum(sel, axis=-1, keepdims=True)
    return idx, wts


def moe(x, xn, logits, router_bias, w_gate, w_up, w_down, layer):
    t, d = x.shape
    n_experts = w_gate.shape[1]
    idx, wts = _route(logits, router_bias, n_experts)
    na = t * TOP_K
    ntile = -(-na // MOE_TILE) + n_experts
    pos_col, tile_end = dispatch(idx.T.reshape(na, 1).astype(jnp.int32), n_experts)
    pos = pos_col.reshape(TOP_K, t)
    tile_end = tile_end[0, :n_experts]
    tok = jnp.tile(jnp.arange(t, dtype=jnp.int32), TOP_K)
    row_token = jnp.zeros((ntile * MOE_TILE,), jnp.int32).at[pos_col[:, 0]].set(tok)
    row_w = jnp.zeros((ntile * MOE_TILE, 1), F32).at[pos_col[:, 0], 0].set(wts.T.reshape(-1))
    n_tiles = tile_end[-1:]
    tile_ids = jnp.arange(ntile, dtype=jnp.int32)
    tile_e = jnp.minimum(jnp.searchsorted(tile_end, tile_ids, side="right"), n_experts - 1).astype(jnp.int32)
    tile_first = jnp.concatenate([jnp.ones((1,), jnp.int32),
                                  (tile_e[1:] != tile_e[:-1]).astype(jnp.int32)])

    gather = lambda a, ix: a.at[ix].get(mode="promise_in_bounds")
    x_sorted = gather(xn, row_token)
    h_sorted = moe_up(x_sorted, row_w, w_gate, w_up, layer, tile_e, tile_first, n_tiles)
    o_sorted = moe_down(h_sorted, w_down, layer, tile_e, tile_first, n_tiles)
    y = gather(o_sorted, pos[0])
    for kk in range(1, TOP_K):
        y = y + gather(o_sorted, pos[kk])
    return x + y


def kernel(x_prompt, x_sample, mem_prompt, cache_mem_k, cache_mem_v, state_sconv, state_ssm_conv, state_ssm,
           norm_mix, w_in, sconv_w, w_sconv_out, ssd_conv_w, ssd_conv_b, ssd_dt_bias, ssd_a_log, ssd_d, ssd_norm,
           w_ssd_out, w_mix_out, norm_xattn, norm_mem, w_xk, w_xv, w_xq, w_xo, norm_moe, w_router, router_bias,
           w_gate, w_up, w_down, norm_final):
    nb, seq, d = x_prompt.shape
    ts = x_sample.shape[0]
    depth = w_in.shape[0]
    tp = nb * seq
    t = tp + ts
    c = sconv_w.shape[-1]
    d_inner = ssd_norm.shape[-1]
    heads = ssd_dt_bias.shape[-1]
    nstate = state_ssm.shape[-1]
    ch = ssd_conv_w.shape[-1]
    groups = (ch - d_inner) // (2 * nstate)
    hpg = heads // groups
    n_mem, nh, hd = cache_mem_k.shape[2:]
    xw = nh * hd
    assert x_sample.shape[1] == 1 and c == d and d_inner == 2 * d and heads * SSD_HEAD_DIM == d_inner
    assert ts <= seq and tp % ts == 0 and heads % 128 == 0

    col_z = 3 * c
    col_xbc = col_z + d_inner
    col_dt = col_xbc + ch
    col_g = col_dt + heads

    tm = _tile(t, 1040, 16)
    tn = 512
    mmk = functools.partial(mm, k=d, tm=tm)
    sig = lambda acc: jax.nn.sigmoid(acc)

    onehot = (jnp.arange(heads)[:, None] == (jnp.arange(d_inner) // SSD_HEAD_DIM)[None, :]).astype(BF16)
    mem2 = mem_prompt.reshape(nb * n_mem, d)

    x = jnp.concatenate([x_prompt.reshape(tp, d), x_sample.reshape(ts, d)], axis=0)
    outs = {n: [] for n in ("mk", "mv", "scp", "ccp", "hp", "scs", "ccs")}
    hs = None

    for l in range(depth):
        xn = rmsnorm(x, norm_mix[l], BF16)
        c3 = mmk(xn, w_in, (l,), col0=0, ncols=3 * c, tn=tn, name="in_conv")
        z = mmk(xn, w_in, (l,), col0=col_z, ncols=d_inner, tn=tn, name="in_z")
        xraw = mmk(xn, w_in, (l,), col0=col_xbc, ncols=ch, tn=tn, name="in_xbc")
        dtraw = mmk(xn, w_in, (l,), col0=col_dt, ncols=heads, tn=heads, name="in_dt")
        gates = mmk(xn, w_in, (l,), col0=col_g, ncols=2 * d, tn=tn, epilogue=sig, name="in_gates")

        sc_state = jnp.moveaxis(state_sconv[l], 1, 0)
        v, scp, u_new = sconv(c3, sconv_w[l], sc_state, nb=nb, seq=seq, ts=ts)
        cc_state = jnp.moveaxis(state_ssm_conv[l], 1, 0)
        cargs = dict(nb=nb, seq=seq, ts=ts)
        xs, ccp_x, xs_s = xconv(xraw, ssd_conv_w[l], ssd_conv_b[l], cc_state, col0=0, ncols=d_inner,
                                out_dtype=F32, **cargs)
        bc, ccp_bc, bc_s = xconv(xraw, ssd_conv_w[l], ssd_conv_b[l], cc_state, col0=d_inner,
                                 ncols=ch - d_inner, out_dtype=BF16, **cargs)
        ccp = jnp.concatenate([ccp_x, ccp_bc], axis=-1)

        dsk_c = jnp.repeat(ssd_d[l], SSD_HEAD_DIM)
        yn, hp = ssd_prompt(xs, bc, dtraw, z, ssd_norm[l], ssd_dt_bias[l], ssd_a_log[l], dsk_c, onehot,
                            nb=nb, seq=seq, t=t, d_inner=d_inner, groups=groups, nstate=nstate)

        bm_s = bc_s[:, :groups * nstate].reshape(ts, groups, nstate)
        cm_s = bc_s[:, groups * nstate:].reshape(ts, groups, nstate)
        hs, y_t = ssd_sample(state_ssm.reshape(depth, ts, d_inner, nstate), l, hs, dtraw[tp:].T, xs_s.T,
                             jnp.moveaxis(bm_s, 1, 0), jnp.transpose(bm_s, (1, 2, 0)),
                             jnp.transpose(cm_s, (1, 2, 0)),
                             ssd_dt_bias[l], ssd_a_log[l], ssd_d[l], hpg=hpg)
        yn = gated_norm_sample(y_t.T, z, ssd_norm[l], yn, tp=tp, groups=groups)

        y_conv = mmk(v, w_sconv_out, (l,), ncols=d, tn=tn, extras=[(gates, 0)],
                     epilogue=lambda acc, gc_: acc * gc_, name="sconv_out")
        y_ssd0 = mmk(yn, w_ssd_out, (l,), ncols=d, tn=tn, kb_a=0, kb_w=0, name="ssd_out0")
        mixed = mmk(yn, w_ssd_out, (l,), ncols=d, tn=tn, kb_a=1, kb_w=1,
                    extras=[(y_ssd0, 0), (y_conv, 0), (gates, d // tn)],
                    epilogue=lambda acc, y0, yc, gs: yc + gs * (y0 + acc), out_dtype=BF16, name="ssd_out1")
        x = mmk(mixed, w_mix_out, (l,), ncols=d, tn=tn, extras=[(x, 0)],
                epilogue=lambda acc, xr: xr + acc, name="mix_out")

        memn = rmsnorm(mem2, norm_mem[l], BF16)
        tmm = _tile(nb * n_mem, 1024, 16)
        k_p = mm(memn, w_xk, (l,), k=d, ncols=xw, tn=tn, tm=tmm, name="mem_k")
        v_p = mm(memn, w_xv, (l,), k=d, ncols=xw, tn=tn, tm=tmm, name="mem_v")
        xn = rmsnorm(x, norm_xattn[l], BF16)
        qa = mmk(xn, w_xq, (l,), ncols=xw, tn=tn, out_dtype=BF16, name="xq")
        o_p = attention(qa[:tp].reshape(nb, seq, xw), k_p.reshape(nb, n_mem, xw), v_p.reshape(nb, n_mem, xw),
                        nh=nh, hd=hd)
        q_s = jnp.broadcast_to(qa[tp:].reshape(ts, 1, xw), (ts, 8, xw))
        o_s = attention(q_s, cache_mem_k[l].reshape(ts, n_mem, xw), cache_mem_v[l].reshape(ts, n_mem, xw),
                        nh=nh, hd=hd)
        o = jnp.concatenate([o_p.reshape(tp, xw), o_s[:, 0]], axis=0)
        x = mm(o, w_xo, (l,), k=xw, ncols=d, tn=1024, tm=tm, extras=[(x, 0)],
               epilogue=lambda acc, xr: xr + acc, name="xo")

        xn, logits = router(x, norm_moe[l], w_router)
        x = moe(x, xn, logits, router_bias, w_gate, w_up, w_down, l)

        outs["mk"].append(k_p.reshape(nb, n_mem, nh, hd))
        outs["mv"].append(v_p.reshape(nb, n_mem, nh, hd))
        outs["scp"].append(scp)
        outs["ccp"].append(ccp)
        outs["hp"].append(hp)
        outs["scs"].append(jnp.stack([state_sconv[l][:, 1], u_new], axis=1))
        outs["ccs"].append(jnp.concatenate([state_ssm_conv[l][:, 1:], xraw[tp:, None, :]], axis=1))

    y = rmsnorm(x, norm_final, F32)
    st = lambda n: jnp.stack(outs[n])
    return (y[:tp].reshape(nb, seq, d), y[tp:].reshape(ts, 1, d), st("mk"), st("mv"), st("scp"), st("ccp"),
            st("hp"), st("scs"), st("ccs"), hs.reshape(depth, ts, heads, SSD_HEAD_DIM, nstate))
```

```python
import functools

import jax
import jax.numpy as jnp
from jax import lax
from jax.experimental import pallas as pl
from jax.experimental.pallas import tpu as pltpu

F32 = jnp.float32
BF16 = jnp.bfloat16
EPS = 1e-6
SSD_HEAD_DIM = 64
SSD_CHUNK = 128
N_EXPERT_GROUPS = 8
TOP_K = 2
MOE_TILE = 256
MIB = 1024 * 1024
VMEM_CAP = 60 * MIB


def _tile(n, pref, mult):
    best = None
    for d in range(mult, min(n, pref) + 1, mult):
        if n % d == 0:
            best = d
    return best if best is not None else n


def _params(sem, vmem_bytes):
    return pltpu.CompilerParams(dimension_semantics=sem,
                                vmem_limit_bytes=int(min(max(vmem_bytes, 16 * MIB), VMEM_CAP)))


def _silu(x):
    return x * jax.nn.sigmoid(x)


def _softplus(x):
    return jnp.maximum(x, 0.0) + jnp.log1p(jnp.exp(-jnp.abs(x)))


def _rms_kernel(x_ref, g_ref, o_ref):
    x = x_ref[...]
    y = x * lax.rsqrt(jnp.mean(x * x, axis=-1, keepdims=True) + EPS)
    o_ref[...] = (y * g_ref[...]).astype(o_ref.dtype)


def rmsnorm(x, g, out_dtype):
    t, d = x.shape
    tr = _tile(t, 512, 16)
    return pl.pallas_call(
        _rms_kernel,
        grid=(t // tr,),
        in_specs=[pl.BlockSpec((tr, d), lambda i: (i, 0)),
                  pl.BlockSpec((1, d), lambda i: (0, 0))],
        out_specs=pl.BlockSpec((tr, d), lambda i: (i, 0)),
        out_shape=jax.ShapeDtypeStruct((t, d), out_dtype),
        compiler_params=_params(("arbitrary",), 6 * tr * d * 4),
        name="rmsnorm",
    )(x, g.reshape(1, d))


def _cast_dot(a_ref, w_ref, wbf_ref, chunks):
    k = wbf_ref.shape[0]
    ck = k // chunks
    lead = (0,) * (len(w_ref.shape) - 2)
    acc = None
    for kc in range(chunks):
        sl = slice(kc * ck, (kc + 1) * ck)
        wbf_ref[sl, :] = w_ref[lead + (sl, slice(None))].astype(BF16)
        part = jnp.dot(a_ref[:, sl], wbf_ref[sl, :], preferred_element_type=F32)
        acc = part if acc is None else acc + part
    return acc


def _mm_kernel(a_ref, w_ref, *rest, n_extra, epilogue, chunks):
    extra = rest[:n_extra]
    o_ref = rest[n_extra]
    wbf_ref = rest[n_extra + 1]

    def finish(acc):
        if epilogue is not None:
            acc = epilogue(acc, *[e[...] for e in extra])
        o_ref[...] = acc.astype(o_ref.dtype)

    @pl.when(pl.program_id(1) == 0)
    def _():
        finish(_cast_dot(a_ref, w_ref, wbf_ref, chunks))

    @pl.when(pl.program_id(1) != 0)
    def _():
        finish(jnp.dot(a_ref[...], wbf_ref[...], preferred_element_type=F32))


def mm(a, w, lead, *, k, ncols, tn, tm, kb_w=0, kb_a=0, col0=0, extras=(), epilogue=None,
       out_dtype=F32, name="mm"):
    m = a.shape[0]
    assert m % tm == 0 and ncols % tn == 0
    nj, ni = ncols // tn, m // tm
    lead = tuple(lead)
    if col0 % tn == 0:
        jb0 = col0 // tn
        w_spec = pl.BlockSpec((None,) * len(lead) + (k, tn), lambda j, i: lead + (kb_w, jb0 + j))
    else:
        w_spec = pl.BlockSpec((pl.Element(1),) * len(lead) + (pl.Element(k), pl.Element(tn)),
                              lambda j, i: lead + (kb_w * k, pl.multiple_of(col0 + j * tn, 128)))
    in_specs = [pl.BlockSpec((tm, k), lambda j, i: (i, kb_a)), w_spec]
    for _, off in extras:
        in_specs.append(pl.BlockSpec((tm, tn), lambda j, i, off=off: (i, off + j)))
    out_bytes = jnp.dtype(out_dtype).itemsize
    vmem = (2 * k * tn * 4 + k * tn * 2 + 2 * tm * k * 2 + 2 * tm * tn * out_bytes
            + sum(2 * tm * tn * e.dtype.itemsize for e, _ in extras) + 3 * tm * tn * 4 + 2 * MIB)
    return pl.pallas_call(
        functools.partial(_mm_kernel, n_extra=len(extras), epilogue=epilogue, chunks=max(1, k // 512)),
        grid=(nj, ni),
        in_specs=in_specs,
        out_specs=pl.BlockSpec((tm, tn), lambda j, i: (i, j)),
        out_shape=jax.ShapeDtypeStruct((m, ncols), out_dtype),
        scratch_shapes=[pltpu.VMEM((k, tn), BF16)],
        compiler_params=_params(("arbitrary", "arbitrary"), vmem),
        name=name,
    )(a, w, *[e for e, _ in extras])


def _shift_rows(x, s):
    row = lax.broadcasted_iota(jnp.int32, x.shape, 0)
    return jnp.where(row >= s, pltpu.roll(x, s, 0), 0.0)


def _sconv_kernel(cb_ref, cc_ref, cx_ref, w_ref, s_ref, v_ref, stp_ref, un_ref, *, nb, seq, ts):
    b = pl.program_id(1)
    w = w_ref[...]

    @pl.when(b < nb)
    def _():
        u = cc_ref[...] * cx_ref[...]
        y = w[0:1] * _shift_rows(u, 2) + w[1:2] * _shift_rows(u, 1) + w[2:3] * u
        v_ref[...] = (cb_ref[...] * y).astype(v_ref.dtype)
        stp_ref[...] = u[seq - 2:seq, :]

    @pl.when(b == nb)
    def _():
        u = cc_ref[0:ts, :] * cx_ref[0:ts, :]
        y = w[0:1] * s_ref[0] + w[1:2] * s_ref[1] + w[2:3] * u
        v_ref[0:ts, :] = (cb_ref[0:ts, :] * y).astype(v_ref.dtype)
        un_ref[...] = u


def sconv(c3, w, state_t, *, nb, seq, ts):
    t = c3.shape[0]
    c = c3.shape[1] // 3
    tc = _tile(c, 256, 128)
    ncb = c // tc
    kern = functools.partial(_sconv_kernel, nb=nb, seq=seq, ts=ts)
    return pl.pallas_call(
        kern,
        grid=(ncb, nb + 1),
        in_specs=[pl.BlockSpec((seq, tc), lambda j, b: (b, j)),
                  pl.BlockSpec((seq, tc), lambda j, b: (b, ncb + j)),
                  pl.BlockSpec((seq, tc), lambda j, b: (b, 2 * ncb + j)),
                  pl.BlockSpec((3, tc), lambda j, b: (0, j)),
                  pl.BlockSpec((2, ts, tc), lambda j, b: (0, 0, j))],
        out_specs=[pl.BlockSpec((seq, tc), lambda j, b: (b, j)),
                   pl.BlockSpec((None, 2, tc), lambda j, b: (jnp.minimum(b, nb - 1), 0, j)),
                   pl.BlockSpec((ts, tc), lambda j, b: (0, j))],
        out_shape=[jax.ShapeDtypeStruct((t, c), BF16),
                   jax.ShapeDtypeStruct((nb, 2, c), F32),
                   jax.ShapeDtypeStruct((ts, c), F32)],
        compiler_params=_params(("arbitrary", "arbitrary"), 16 * seq * tc * 4),
        name="sconv",
    )(c3, c3, c3, w, state_t)


def _xconv_kernel(x_ref, w_ref, bias_ref, s_ref, o_ref, stp_ref, os_ref, *, nb, seq, ts):
    b = pl.program_id(1)
    w = w_ref[...]
    bias = bias_ref[...]

    @pl.when(b < nb)
    def _():
        x = x_ref[...]
        y = (w[0:1] * _shift_rows(x, 3) + w[1:2] * _shift_rows(x, 2) + w[2:3] * _shift_rows(x, 1)
             + w[3:4] * x + bias)
        o_ref[...] = _silu(y).astype(o_ref.dtype)
        stp_ref[...] = x[seq - 3:seq, :]

    @pl.when(b == nb)
    def _():
        x = x_ref[0:ts, :]
        y = w[0:1] * s_ref[0] + w[1:2] * s_ref[1] + w[2:3] * s_ref[2] + w[3:4] * x + bias
        act = _silu(y)
        o_ref[0:ts, :] = act.astype(o_ref.dtype)
        os_ref[...] = act


def xconv(xraw, w, bias, state_t, *, col0, ncols, out_dtype, nb, seq, ts):
    t, ch = xraw.shape
    tc = _tile(ncols, 256, 128)
    assert col0 % tc == 0
    jb0 = col0 // tc
    kern = functools.partial(_xconv_kernel, nb=nb, seq=seq, ts=ts)
    return pl.pallas_call(
        kern,
        grid=(ncols // tc, nb + 1),
        in_specs=[pl.BlockSpec((seq, tc), lambda j, b: (b, jb0 + j)),
                  pl.BlockSpec((4, tc), lambda j, b: (0, jb0 + j)),
                  pl.BlockSpec((1, tc), lambda j, b: (0, jb0 + j)),
                  pl.BlockSpec((3, ts, tc), lambda j, b: (0, 0, jb0 + j))],
        out_specs=[pl.BlockSpec((seq, tc), lambda j, b: (b, j)),
                   pl.BlockSpec((None, 3, tc), lambda j, b: (jnp.minimum(b, nb - 1), 0, j)),
                   pl.BlockSpec((ts, tc), lambda j, b: (0, j))],
        out_shape=[jax.ShapeDtypeStruct((t, ncols), out_dtype),
                   jax.ShapeDtypeStruct((nb, 3, ncols), F32),
                   jax.ShapeDtypeStruct((ts, ncols), F32)],
        compiler_params=_params(("arbitrary", "arbitrary"), 16 * seq * tc * 4),
        name="xconv",
    )(xraw, w, bias.reshape(1, ch), state_t)


def _cumsum_rows(x):
    n = x.shape[0]
    row = lax.broadcasted_iota(jnp.int32, x.shape, 0)
    s = 1
    while s < n:
        x = x + jnp.where(row >= s, pltpu.roll(x, s, 0), 0.0)
        s *= 2
    return x


def _expand_heads(v, onehot):
    v1 = v.astype(BF16)
    r1 = v - v1.astype(F32)
    v2 = r1.astype(BF16)
    v3 = (r1 - v2.astype(F32)).astype(BF16)
    d = functools.partial(jnp.dot, preferred_element_type=F32)
    return d(v1, onehot) + d(v2, onehot) + d(v3, onehot)


def _ssd_prompt_kernel(xs_ref, bm_ref, cm_ref, dt_ref, z_ref, nrm_ref, dtb_ref, alog_ref, dsk_ref,
                       oh_ref, yn_ref, hfin_ref, st_ref, act_ref, *, hpg, nchunk):
    g = pl.program_id(1)
    c = pl.program_id(2)
    q = xs_ref.shape[0]
    p = SSD_HEAD_DIM

    @pl.when(c == 0)
    def _():
        st_ref[...] = jnp.zeros_like(st_ref)

    dt_h = _softplus(dt_ref[...] + dtb_ref[...])
    a_h = -jnp.exp(alog_ref[...])
    acum_h = _cumsum_rows(dt_h * a_h)
    act_ref[...] = acum_h.T
    onehot = oh_ref[...]
    dt_c = _expand_heads(dt_h, onehot)
    acum = _expand_heads(acum_h, onehot)

    xs = xs_ref[...]
    bm = bm_ref[...]
    cm = cm_ref[...]
    xdt = (xs * dt_c).astype(BF16)
    cb = lax.dot_general(cm, bm, (((1,), (1,)), ((), ())), preferred_element_type=F32)
    li = lax.broadcasted_iota(jnp.int32, (q, q), 0)
    si = lax.broadcasted_iota(jnp.int32, (q, q), 1)
    causal = li >= si
    first_half = si < p

    parts = []
    for pr in range(hpg // 2):
        rhs = xdt[:, pr * 2 * p:(pr + 1) * 2 * p]
        res = []
        for kk in range(2):
            hl = 2 * pr + kk
            col = acum[:, hl * p:hl * p + 1]
            row = act_ref[pl.ds(g * hpg + hl, 1), :]
            dec = jnp.exp(jnp.where(causal, col - row, -jnp.inf))
            res.append(jnp.dot((cb * dec).astype(BF16), rhs, preferred_element_type=F32))
        parts.append(jnp.where(first_half, res[0], res[1]))
    y = jnp.concatenate(parts, axis=1)

    st = st_ref[...]
    y = y + jnp.dot(cm, st.astype(BF16), preferred_element_type=F32) * jnp.exp(acum)
    alast = acum[q - 1:q, :]
    xw = (xs * dt_c * jnp.exp(alast - acum)).astype(BF16)
    bm_t = bm.astype(F32).T.astype(BF16)
    st_new = jnp.exp(alast) * st + jnp.dot(bm_t, xw, preferred_element_type=F32)
    st_ref[...] = st_new

    @pl.when(c == nchunk - 1)
    def _():
        hfin_ref[...] = st_new.T.reshape(hfin_ref.shape)

    y = y + dsk_ref[...] * xs
    v = y * _silu(z_ref[...])
    v = v * lax.rsqrt(jnp.mean(v * v, axis=-1, keepdims=True) + EPS)
    yn_ref[...] = (v * nrm_ref[...]).astype(yn_ref.dtype)


def ssd_prompt(xs, bc, dtraw, z, nrm, dtb, alog, dsk_c, onehot, *, nb, seq, t, d_inner, groups, nstate):
    heads = d_inner // SSD_HEAD_DIM
    hpg = heads // groups
    gc = d_inner // groups
    q = SSD_CHUNK if seq % SSD_CHUNK == 0 else seq
    nchunk = seq // q
    assert nstate == 128 and q % 8 == 0 and gc % 128 == 0 and hpg % 2 == 0
    kern = functools.partial(_ssd_prompt_kernel, hpg=hpg, nchunk=nchunk)
    rb = lambda b, g, c: b * nchunk + c
    return pl.pallas_call(
        kern,
        grid=(nb, groups, nchunk),
        in_specs=[pl.BlockSpec((q, gc), lambda b, g, c: (rb(b, g, c), g)),
                  pl.BlockSpec((q, nstate), lambda b, g, c: (rb(b, g, c), g)),
                  pl.BlockSpec((q, nstate), lambda b, g, c: (rb(b, g, c), groups + g)),
                  pl.BlockSpec((q, heads), lambda b, g, c: (rb(b, g, c), 0)),
                  pl.BlockSpec((q, gc), lambda b, g, c: (rb(b, g, c), g)),
                  pl.BlockSpec((1, gc), lambda b, g, c: (0, g)),
                  pl.BlockSpec((1, heads), lambda b, g, c: (0, 0)),
                  pl.BlockSpec((1, heads), lambda b, g, c: (0, 0)),
                  pl.BlockSpec((1, gc), lambda b, g, c: (0, g)),
                  pl.BlockSpec((heads, gc), lambda b, g, c: (0, g))],
        out_specs=[pl.BlockSpec((q, gc), lambda b, g, c: (rb(b, g, c), g)),
                   pl.BlockSpec((None, hpg, SSD_HEAD_DIM, nstate), lambda b, g, c: (b, g, 0, 0))],
        out_shape=[jax.ShapeDtypeStruct((t, d_inner), BF16),
                   jax.ShapeDtypeStruct((nb, heads, SSD_HEAD_DIM, nstate), F32)],
        scratch_shapes=[pltpu.VMEM((nstate, gc), F32), pltpu.VMEM((heads, q), F32)],
        compiler_params=_params(("arbitrary", "arbitrary", "arbitrary"), 32 * MIB),
        name="ssd_prompt",
    )(xs, bc, bc, dtraw, z, nrm.reshape(1, d_inner), dtb.reshape(1, heads), alog.reshape(1, heads),
      dsk_c.reshape(1, d_inner), onehot)


def _ssd_sample_kernel(dtb_ref, alog_ref, dsk_ref, h0_ref, dtt_ref, xst_ref, bm_ref, bmt_ref, cmt_ref,
                       *rest, ts):
    hn_ref, yt_ref = rest[-2:]
    k = pl.program_id(0)
    p = SSD_HEAD_DIM
    dt = _softplus(dtt_ref[pl.ds(k, 1), :] + dtb_ref[k])
    a = -jnp.exp(jnp.full((1, ts), alog_ref[k], F32))
    dec_t = jnp.broadcast_to(jnp.exp(dt * a), (p, ts))
    xs_t = xst_ref[...]
    xdt_t = xs_t * dt
    for j in range(ts):
        hn_ref[j] = h0_ref[j] * dec_t[:, j:j + 1] + xdt_t[:, j:j + 1] * bm_ref[j:j + 1, :]
    h0 = h0_ref[...]
    cm_t = cmt_ref[...]
    res = jnp.dot(h0.reshape(ts * p, h0.shape[2]).astype(BF16), cm_t.astype(BF16),
                  preferred_element_type=F32).reshape(ts, p, ts)
    jj = lax.broadcasted_iota(jnp.int32, res.shape, 0)
    ll = lax.broadcasted_iota(jnp.int32, res.shape, 2)
    y_off = jnp.sum(jnp.where(jj == ll, res, 0.0), axis=0)
    cb = jnp.sum(cm_t * bmt_ref[...], axis=0, keepdims=True)
    yt_ref[...] = cb * xdt_t + y_off * dec_t + dsk_ref[k] * xs_t


def ssd_sample(h0_all, layer, hn_prev, dtraw_t, xs_t, bm, bm_t, cm_t, dtb, alog, dsk, *, hpg):
    depth, ts, rows, nstate = h0_all.shape
    p = SSD_HEAD_DIM
    heads = rows // p
    smem = pl.BlockSpec(memory_space=pltpu.SMEM)
    in_specs = [smem, smem, smem,
                pl.BlockSpec((None, ts, p, nstate), lambda k: (layer, 0, k, 0)),
                pl.BlockSpec((heads, ts), lambda k: (0, 0)),
                pl.BlockSpec((p, ts), lambda k: (k, 0)),
                pl.BlockSpec((None, ts, nstate), lambda k: (k // hpg, 0, 0)),
                pl.BlockSpec((None, nstate, ts), lambda k: (k // hpg, 0, 0)),
                pl.BlockSpec((None, nstate, ts), lambda k: (k // hpg, 0, 0))]
    args = [dtb, alog, dsk, h0_all, dtraw_t, xs_t, bm, bm_t, cm_t]
    aliases = {}
    if hn_prev is not None:
        in_specs.append(pl.BlockSpec(memory_space=pl.ANY))
        args.append(hn_prev)
        aliases = {len(args) - 1: 0}
    return pl.pallas_call(
        functools.partial(_ssd_sample_kernel, ts=ts),
        grid=(heads,),
        in_specs=in_specs,
        out_specs=[pl.BlockSpec((None, ts, p, nstate), lambda k: (layer, 0, k, 0)),
                   pl.BlockSpec((p, ts), lambda k: (k, 0))],
        out_shape=[jax.ShapeDtypeStruct((depth, ts, rows, nstate), F32),
                   jax.ShapeDtypeStruct((rows, ts), F32)],
        input_output_aliases=aliases,
        compiler_params=_params(("arbitrary",), 40 * MIB),
        name="ssd_sample",
    )(*args)


def _gnorm_kernel(y_ref, z_ref, nrm_ref, yn_in_ref, o_ref):
    del yn_in_ref
    v = y_ref[...] * _silu(z_ref[...])
    v = v * lax.rsqrt(jnp.mean(v * v, axis=-1, keepdims=True) + EPS)
    o_ref[...] = (v * nrm_ref[...]).astype(o_ref.dtype)


def gated_norm_sample(y_s, z, nrm, yn, *, tp, groups):
    ts, d_inner = y_s.shape
    gc = d_inner // groups
    assert tp % ts == 0
    rb = tp // ts
    return pl.pallas_call(
        _gnorm_kernel,
        grid=(groups,),
        in_specs=[pl.BlockSpec((ts, gc), lambda g: (0, g)),
                  pl.BlockSpec((ts, gc), lambda g: (rb, g)),
                  pl.BlockSpec((1, gc), lambda g: (0, g)),
                  pl.BlockSpec(memory_space=pl.ANY)],
        out_specs=pl.BlockSpec((ts, gc), lambda g: (rb, g)),
        out_shape=jax.ShapeDtypeStruct(yn.shape, yn.dtype),
        input_output_aliases={3: 0},
        compiler_params=_params(("arbitrary",), 16 * MIB),
        name="gated_norm_sample",
    )(y_s, z, nrm.reshape(1, d_inner), yn)


def _attn_kernel(q_ref, k_ref, v_ref, o_ref, *, nh, hd):
    q = q_ref[...]
    k = k_ref[...].astype(BF16)
    v = v_ref[...].astype(BF16)
    scale = hd ** -0.5
    for h in range(nh):
        sl = slice(h * hd, (h + 1) * hd)
        s = lax.dot_general(q[:, sl], k[:, sl], (((1,), (1,)), ((), ())),
                            preferred_element_type=F32) * scale
        e = jnp.exp(s - jnp.max(s, axis=-1, keepdims=True))
        pr = e / jnp.sum(e, axis=-1, keepdims=True)
        o = jnp.dot(pr.astype(BF16), v[:, sl], preferred_element_type=F32)
        o_ref[:, sl] = o.astype(o_ref.dtype)


def attention(q, k, v, *, nh, hd):
    b, lq, w = q.shape
    m = k.shape[1]
    tq = _tile(lq, 512, 8)
    return pl.pallas_call(
        functools.partial(_attn_kernel, nh=nh, hd=hd),
        grid=(b, lq // tq),
        in_specs=[pl.BlockSpec((None, tq, w), lambda bi, i: (bi, i, 0)),
                  pl.BlockSpec((None, m, w), lambda bi, i: (bi, 0, 0)),
                  pl.BlockSpec((None, m, w), lambda bi, i: (bi, 0, 0))],
        out_specs=pl.BlockSpec((None, tq, w), lambda bi, i: (bi, i, 0)),
        out_shape=jax.ShapeDtypeStruct((b, lq, w), BF16),
        compiler_params=_params(("arbitrary", "arbitrary"), 24 * MIB),
        name="attention",
    )(q, k, v)


def _first_argmax(vals):
    best, arg = vals[0], jnp.zeros(vals[0].shape, jnp.int32)
    for i in range(1, len(vals)):
        better = vals[i] > best
        best = jnp.where(better, vals[i], best)
        arg = jnp.where(better, i, arg)
    return best, arg


def _pick(vals, arg):
    out = vals[0]
    for i in range(1, len(vals)):
        out = jnp.where(arg == i, vals[i], out)
    return out


def _router_kernel(x_ref, g_ref, wt_ref, bias_ref, xn_ref, idx_ref, wts_ref, *, n_experts):
    x = x_ref[...]
    xn = (x * lax.rsqrt(jnp.mean(x * x, axis=-1, keepdims=True) + EPS) * g_ref[...]).astype(BF16)
    xn_ref[...] = xn
    logits = lax.dot_general(wt_ref[...], xn, (((1,), (1,)), ((), ())), preferred_element_type=F32)
    scores = jax.nn.sigmoid(logits)
    biased = scores + bias_ref[...]
    per = n_experts // N_EXPERT_GROUPS
    assert per == 4 and TOP_K == 2
    row = lambda a, e: a[e:e + 1, :]
    gscore = []
    for gi in range(N_EXPERT_GROUPS):
        v = [row(biased, gi * per + i) for i in range(per)]
        a, b = jnp.maximum(v[0], v[1]), jnp.minimum(v[0], v[1])
        c, d = jnp.maximum(v[2], v[3]), jnp.minimum(v[2], v[3])
        gscore.append(jnp.maximum(a, c) + jnp.maximum(jnp.minimum(a, c), jnp.maximum(b, d)))
    _, grp = _first_argmax(gscore)
    bsel = [_pick([row(biased, gi * per + i) for gi in range(N_EXPERT_GROUPS)], grp) for i in range(per)]
    ssel = [_pick([row(scores, gi * per + i) for gi in range(N_EXPERT_GROUPS)], grp) for i in range(per)]
    _, i1 = _first_argmax(bsel)
    _, i2 = _first_argmax([jnp.where(i1 == i, -jnp.inf, bsel[i]) for i in range(per)])
    s1, s2 = _pick(ssel, i1), _pick(ssel, i2)
    tot = s1 + s2
    idx_ref[...] = jnp.concatenate([grp * per + i1, grp * per + i2], axis=0)
    wts_ref[...] = jnp.concatenate([s1 / tot, s2 / tot], axis=0)


def router(x, g, w_router, router_bias):
    t, d = x.shape
    e = w_router.shape[1]
    tr = _tile(t, 640, 128)
    return pl.pallas_call(
        functools.partial(_router_kernel, n_experts=e),
        grid=(t // tr,),
        in_specs=[pl.BlockSpec((tr, d), lambda i: (i, 0)),
                  pl.BlockSpec((1, d), lambda i: (0, 0)),
                  pl.BlockSpec((e, d), lambda i: (0, 0)),
                  pl.BlockSpec((e, 1), lambda i: (0, 0))],
        out_specs=[pl.BlockSpec((tr, d), lambda i: (i, 0)),
                   pl.BlockSpec((TOP_K, tr), lambda i: (0, i)),
                   pl.BlockSpec((TOP_K, tr), lambda i: (0, i))],
        out_shape=[jax.ShapeDtypeStruct((t, d), BF16),
                   jax.ShapeDtypeStruct((TOP_K, t), jnp.int32),
                   jax.ShapeDtypeStruct((TOP_K, t), F32)],
        compiler_params=_params(("arbitrary",), 8 * tr * d * 4 + 8 * MIB),
        name="router",
    )(x, g.reshape(1, d), w_router.T.astype(BF16), router_bias.astype(F32).reshape(e, 1))


def _moe_up_kernel(te_ref, tf_ref, nt_ref, x_ref, rw_ref, wg_ref, wu_ref, h_ref, wgb_ref, wub_ref):
    t = pl.program_id(1)

    def finish(gate, up):
        h_ref[...] = (_silu(gate) * up * rw_ref[...]).astype(h_ref.dtype)

    active = t < nt_ref[0]
    first = tf_ref[t] == 1
    chunks = max(1, wgb_ref.shape[0] // 512)

    @pl.when(active & first)
    def _():
        finish(_cast_dot(x_ref, wg_ref, wgb_ref, chunks), _cast_dot(x_ref, wu_ref, wub_ref, chunks))

    @pl.when(active & jnp.logical_not(first))
    def _():
        x = x_ref[...]
        finish(jnp.dot(x, wgb_ref[...], preferred_element_type=F32),
               jnp.dot(x, wub_ref[...], preferred_element_type=F32))

    @pl.when(jnp.logical_not(active))
    def _():
        h_ref[...] = jnp.zeros_like(h_ref)


def moe_up(x_sorted, row_w, w_gate, w_up, layer, tile_e, tile_first, n_tiles):
    pr, d = x_sorted.shape
    f = w_gate.shape[-1]
    tf = _tile(f, 512, 128)
    ntile = pr // MOE_TILE

    def tix(t, nt):
        return jnp.minimum(t, nt[0] - 1)

    grid_spec = pltpu.PrefetchScalarGridSpec(
        num_scalar_prefetch=3,
        grid=(f // tf, ntile),
        in_specs=[pl.BlockSpec((MOE_TILE, d), lambda fb, t, te, tfi, nt: (tix(t, nt), 0)),
                  pl.BlockSpec((MOE_TILE, 1), lambda fb, t, te, tfi, nt: (tix(t, nt), 0)),
                  pl.BlockSpec((None, None, d, tf), lambda fb, t, te, tfi, nt: (layer, te[tix(t, nt)], 0, fb)),
                  pl.BlockSpec((None, None, d, tf), lambda fb, t, te, tfi, nt: (layer, te[tix(t, nt)], 0, fb))],
        out_specs=pl.BlockSpec((MOE_TILE, tf), lambda fb, t, te, tfi, nt: (t, fb)),
        scratch_shapes=[pltpu.VMEM((d, tf), BF16), pltpu.VMEM((d, tf), BF16)],
    )
    vmem = 4 * d * tf * 4 + 2 * d * tf * 2 + 2 * MOE_TILE * d * 2 + 8 * MOE_TILE * tf * 4 + 2 * MIB
    return pl.pallas_call(
        _moe_up_kernel,
        grid_spec=grid_spec,
        out_shape=jax.ShapeDtypeStruct((pr, f), BF16),
        compiler_params=_params(("arbitrary", "arbitrary"), vmem),
        name="moe_up",
    )(tile_e, tile_first, n_tiles, x_sorted, row_w, w_gate, w_up)


def _moe_down_kernel(te_ref, tf_ref, nt_ref, h_ref, wd_ref, o_ref, wdb_ref):
    t = pl.program_id(1)

    active = t < nt_ref[0]
    first = tf_ref[t] == 1

    @pl.when(active & first)
    def _():
        o_ref[...] = _cast_dot(h_ref, wd_ref, wdb_ref, max(1, wdb_ref.shape[0] // 256))

    @pl.when(active & jnp.logical_not(first))
    def _():
        o_ref[...] = jnp.dot(h_ref[...], wdb_ref[...], preferred_element_type=F32)

    @pl.when(jnp.logical_not(active))
    def _():
        o_ref[...] = jnp.zeros_like(o_ref)


def moe_down(h_sorted, w_down, layer, tile_e, tile_first, n_tiles):
    pr, f = h_sorted.shape
    d = w_down.shape[-1]
    tn = _tile(d, 2048, 128)
    ntile = pr // MOE_TILE

    def tix(t, nt):
        return jnp.minimum(t, nt[0] - 1)

    grid_spec = pltpu.PrefetchScalarGridSpec(
        num_scalar_prefetch=3,
        grid=(d // tn, ntile),
        in_specs=[pl.BlockSpec((MOE_TILE, f), lambda nb, t, te, tfi, nt: (tix(t, nt), 0)),
                  pl.BlockSpec((None, None, f, tn), lambda nb, t, te, tfi, nt: (layer, te[tix(t, nt)], 0, nb))],
        out_specs=pl.BlockSpec((MOE_TILE, tn), lambda nb, t, te, tfi, nt: (t, nb)),
        scratch_shapes=[pltpu.VMEM((f, tn), BF16)],
    )
    vmem = 2 * f * tn * 4 + f * tn * 2 + 2 * MOE_TILE * f * 2 + 4 * MOE_TILE * tn * 4 + 2 * MIB
    return pl.pallas_call(
        _moe_down_kernel,
        grid_spec=grid_spec,
        out_shape=jax.ShapeDtypeStruct((pr, d), F32),
        compiler_params=_params(("arbitrary", "arbitrary"), vmem),
        name="moe_down",
    )(tile_e, tile_first, n_tiles, h_sorted, w_down)


def _dispatch_kernel(e_ref, pos_ref, tend_ref, cnt_ref, run_ref, tstart_ref, *, rows):
    ph = pl.program_id(0)
    i = pl.program_id(1)
    lanes = cnt_ref.shape[1]
    onehot = (e_ref[...] == lax.broadcasted_iota(jnp.int32, (rows, lanes), 1)).astype(F32)
    colsum = jnp.sum(onehot, axis=0, keepdims=True)

    @pl.when((ph == 0) & (i == 0))
    def _():
        cnt_ref[...] = jnp.zeros_like(cnt_ref)
        run_ref[...] = jnp.zeros_like(run_ref)

    @pl.when(ph == 0)
    def _():
        cnt_ref[...] += colsum

    @pl.when((ph == 1) & (i == 0))
    def _():
        tiles = jnp.floor((cnt_ref[...] + (MOE_TILE - 1)) * (1.0 / MOE_TILE))
        r = lax.broadcasted_iota(jnp.int32, (lanes, lanes), 0)
        c = lax.broadcasted_iota(jnp.int32, (lanes, lanes), 1)
        before = (r < c).astype(BF16)
        start = jnp.dot(jnp.broadcast_to(tiles, (8, lanes)).astype(BF16), before,
                        preferred_element_type=F32)[0:1]
        tstart_ref[...] = start
        tend_ref[...] = (start + tiles).astype(jnp.int32)

    @pl.when(ph == 1)
    def _():
        r = lax.broadcasted_iota(jnp.int32, (rows, rows), 0)
        c = lax.broadcasted_iota(jnp.int32, (rows, rows), 1)
        earlier = (c < r).astype(BF16)
        rank = jnp.dot(earlier, onehot.astype(BF16), preferred_element_type=F32) + run_ref[...]
        row = tstart_ref[...] * MOE_TILE + rank
        pos_ref[...] = jnp.sum(onehot * row, axis=1, keepdims=True).astype(jnp.int32)
        run_ref[...] += colsum


def dispatch(e_col, n_experts):
    na = e_col.shape[0]
    lanes = -(-n_experts // 128) * 128
    rows = _tile(na, 640, 128)
    assert rows % 128 == 0 and na + n_experts * MOE_TILE < 2 ** 24 and na // MOE_TILE + n_experts <= 256
    nblk = na // rows
    return pl.pallas_call(
        functools.partial(_dispatch_kernel, rows=rows),
        grid=(2, nblk),
        in_specs=[pl.BlockSpec((rows, 1), lambda ph, i: (i, 0))],
        out_specs=[pl.BlockSpec((rows, 1), lambda ph, i: (ph * i, 0)),
                   pl.BlockSpec((1, lanes), lambda ph, i: (0, 0))],
        out_shape=[jax.ShapeDtypeStruct((na, 1), jnp.int32),
                   jax.ShapeDtypeStruct((1, lanes), jnp.int32)],
        scratch_shapes=[pltpu.VMEM((1, lanes), F32), pltpu.VMEM((1, lanes), F32),
                        pltpu.VMEM((1, lanes), F32)],
        compiler_params=_params(("arbitrary", "arbitrary"), 16 * MIB),
        name="dispatch",
    )(e_col)


def moe(x, xn, idx_t, wts_t, w_gate, w_up, w_down, layer):
    t, d = x.shape
    n_experts = w_gate.shape[1]
    na = t * TOP_K
    ntile = -(-na // MOE_TILE) + n_experts
    pos_col, tile_end = dispatch(idx_t.reshape(na, 1), n_experts)
    pos = pos_col.reshape(TOP_K, t)
    tile_end = tile_end[0, :n_experts]
    tok = jnp.tile(jnp.arange(t, dtype=jnp.int32), TOP_K)
    row_token = jnp.zeros((ntile * MOE_TILE,), jnp.int32).at[pos_col[:, 0]].set(tok)
    row_w = jnp.zeros((ntile * MOE_TILE, 1), F32).at[pos_col[:, 0], 0].set(wts_t.reshape(-1))
    n_tiles = tile_end[-1:]
    tile_ids = jnp.arange(ntile, dtype=jnp.int32)
    tile_e = jnp.minimum(jnp.searchsorted(tile_end, tile_ids, side="right"), n_experts - 1).astype(jnp.int32)
    tile_first = jnp.concatenate([jnp.ones((1,), jnp.int32),
                                  (tile_e[1:] != tile_e[:-1]).astype(jnp.int32)])

    gather = lambda a, ix: a.at[ix].get(mode="promise_in_bounds")
    x_sorted = gather(xn, row_token)
    h_sorted = moe_up(x_sorted, row_w, w_gate, w_up, layer, tile_e, tile_first, n_tiles)
    o_sorted = moe_down(h_sorted, w_down, layer, tile_e, tile_first, n_tiles)
    y = gather(o_sorted, pos[0])
    for kk in range(1, TOP_K):
        y = y + gather(o_sorted, pos[kk])
    return x + y


def kernel(x_prompt, x_sample, mem_prompt, cache_mem_k, cache_mem_v, state_sconv, state_ssm_conv, state_ssm,
           norm_mix, w_in, sconv_w, w_sconv_out, ssd_conv_w, ssd_conv_b, ssd_dt_bias, ssd_a_log, ssd_d, ssd_norm,
           w_ssd_out, w_mix_out, norm_xattn, norm_mem, w_xk, w_xv, w_xq, w_xo, norm_moe, w_router, router_bias,
           w_gate, w_up, w_down, norm_final):
    nb, seq, d = x_prompt.shape
    ts = x_sample.shape[0]
    depth = w_in.shape[0]
    tp = nb * seq
    t = tp + ts
    c = sconv_w.shape[-1]
    d_inner = ssd_norm.shape[-1]
    heads = ssd_dt_bias.shape[-1]
    nstate = state_ssm.shape[-1]
    ch = ssd_conv_w.shape[-1]
    groups = (ch - d_inner) // (2 * nstate)
    hpg = heads // groups
    n_mem, nh, hd = cache_mem_k.shape[2:]
    xw = nh * hd
    assert x_sample.shape[1] == 1 and c == d and d_inner == 2 * d and heads * SSD_HEAD_DIM == d_inner
    assert ts <= seq and tp % ts == 0 and heads % 128 == 0

    col_z = 3 * c
    col_xbc = col_z + d_inner
    col_dt = col_xbc + ch
    col_g = col_dt + heads

    tm = _tile(t, 1040, 16)
    tn = 512
    mmk = functools.partial(mm, k=d, tm=tm)
    sig = lambda acc: jax.nn.sigmoid(acc)

    onehot = (jnp.arange(heads)[:, None] == (jnp.arange(d_inner) // SSD_HEAD_DIM)[None, :]).astype(BF16)
    mem2 = mem_prompt.reshape(nb * n_mem, d)

    x = jnp.concatenate([x_prompt.reshape(tp, d), x_sample.reshape(ts, d)], axis=0)
    outs = {n: [] for n in ("mk", "mv", "scp", "ccp", "hp", "scs", "ccs")}
    hs = None

    for l in range(depth):
        xn = rmsnorm(x, norm_mix[l], BF16)
        c3 = mmk(xn, w_in, (l,), col0=0, ncols=3 * c, tn=tn, name="in_conv")
        z = mmk(xn, w_in, (l,), col0=col_z, ncols=d_inner, tn=tn, name="in_z")
        xraw = mmk(xn, w_in, (l,), col0=col_xbc, ncols=ch, tn=tn, name="in_xbc")
        dtraw = mmk(xn, w_in, (l,), col0=col_dt, ncols=heads, tn=heads, name="in_dt")
        gates = mmk(xn, w_in, (l,), col0=col_g, ncols=2 * d, tn=tn, epilogue=sig, name="in_gates")

        sc_state = jnp.moveaxis(state_sconv[l], 1, 0)
        v, scp, u_new = sconv(c3, sconv_w[l], sc_state, nb=nb, seq=seq, ts=ts)
        cc_state = jnp.moveaxis(state_ssm_conv[l], 1, 0)
        cargs = dict(nb=nb, seq=seq, ts=ts)
        xs, ccp_x, xs_s = xconv(xraw, ssd_conv_w[l], ssd_conv_b[l], cc_state, col0=0, ncols=d_inner,
                                out_dtype=F32, **cargs)
        bc, ccp_bc, bc_s = xconv(xraw, ssd_conv_w[l], ssd_conv_b[l], cc_state, col0=d_inner,
                                 ncols=ch - d_inner, out_dtype=BF16, **cargs)
        ccp = jnp.concatenate([ccp_x, ccp_bc], axis=-1)

        dsk_c = jnp.repeat(ssd_d[l], SSD_HEAD_DIM)
        yn, hp = ssd_prompt(xs, bc, dtraw, z, ssd_norm[l], ssd_dt_bias[l], ssd_a_log[l], dsk_c, onehot,
                            nb=nb, seq=seq, t=t, d_inner=d_inner, groups=groups, nstate=nstate)

        bm_s = bc_s[:, :groups * nstate].reshape(ts, groups, nstate)
        cm_s = bc_s[:, groups * nstate:].reshape(ts, groups, nstate)
        hs, y_t = ssd_sample(state_ssm.reshape(depth, ts, d_inner, nstate), l, hs, dtraw[tp:].T, xs_s.T,
                             jnp.moveaxis(bm_s, 1, 0), jnp.transpose(bm_s, (1, 2, 0)),
                             jnp.transpose(cm_s, (1, 2, 0)),
                             ssd_dt_bias[l], ssd_a_log[l], ssd_d[l], hpg=hpg)
        yn = gated_norm_sample(y_t.T, z, ssd_norm[l], yn, tp=tp, groups=groups)

        y_conv = mmk(v, w_sconv_out, (l,), ncols=d, tn=tn, extras=[(gates, 0)],
                     epilogue=lambda acc, gc_: acc * gc_, name="sconv_out")
        y_ssd0 = mmk(yn, w_ssd_out, (l,), ncols=d, tn=tn, kb_a=0, kb_w=0, name="ssd_out0")
        mixed = mmk(yn, w_ssd_out, (l,), ncols=d, tn=tn, kb_a=1, kb_w=1,
                    extras=[(y_ssd0, 0), (y_conv, 0), (gates, d // tn)],
                    epilogue=lambda acc, y0, yc, gs: yc + gs * (y0 + acc), out_dtype=BF16, name="ssd_out1")
        x = mmk(mixed, w_mix_out, (l,), ncols=d, tn=tn, extras=[(x, 0)],
                epilogue=lambda acc, xr: xr + acc, name="mix_out")

        memn = rmsnorm(mem2, norm_mem[l], BF16)
        tmm = _tile(nb * n_mem, 1024, 16)
        k_p = mm(memn, w_xk, (l,), k=d, ncols=xw, tn=tn, tm=tmm, name="mem_k")
        v_p = mm(memn, w_xv, (l,), k=d, ncols=xw, tn=tn, tm=tmm, name="mem_v")
        xn = rmsnorm(x, norm_xattn[l], BF16)
        qa = mmk(xn, w_xq, (l,), ncols=xw, tn=tn, out_dtype=BF16, name="xq")
        o_p = attention(qa[:tp].reshape(nb, seq, xw), k_p.reshape(nb, n_mem, xw), v_p.reshape(nb, n_mem, xw),
                        nh=nh, hd=hd)
        q_s = jnp.broadcast_to(qa[tp:].reshape(ts, 1, xw), (ts, 8, xw))
        o_s = attention(q_s, cache_mem_k[l].reshape(ts, n_mem, xw), cache_mem_v[l].reshape(ts, n_mem, xw),
                        nh=nh, hd=hd)
        o = jnp.concatenate([o_p.reshape(tp, xw), o_s[:, 0]], axis=0)
        x = mm(o, w_xo, (l,), k=xw, ncols=d, tn=1024, tm=tm, extras=[(x, 0)],
               epilogue=lambda acc, xr: xr + acc, name="xo")

        xn, idx_t, wts_t = router(x, norm_moe[l], w_router, router_bias)
        x = moe(x, xn, idx_t, wts_t, w_gate, w_up, w_down, l)

        outs["mk"].append(k_p.reshape(nb, n_mem, nh, hd))
        outs["mv"].append(v_p.reshape(nb, n_mem, nh, hd))
        outs["scp"].append(scp)
        outs["ccp"].append(ccp)
        outs["hp"].append(hp)
        outs["scs"].append(jnp.stack([state_sconv[l][:, 1], u_new], axis=1))
        outs["ccs"].append(jnp.concatenate([state_ssm_conv[l][:, 1:], xraw[tp:, None, :]], axis=1))

    y = rmsnorm(x, norm_final, F32)
    st = lambda n: jnp.stack(outs[n])
    return (y[:tp].reshape(nb, seq, d), y[tp:].reshape(ts, 1, d), st("mk"), st("mv"), st("scp"), st("ccp"),
            st("hp"), st("scs"), st("ccs"), hs.reshape(depth, ts, heads, SSD_HEAD_DIM, nstate))
```

```python
import functools

import jax
import jax.numpy as jnp
from jax import lax
from jax.experimental import pallas as pl
from jax.experimental.pallas import tpu as pltpu

F32 = jnp.float32
BF16 = jnp.bfloat16
EPS = 1e-6
SSD_HEAD_DIM = 64
SSD_CHUNK = 128
N_EXPERT_GROUPS = 8
TOP_K = 2
MOE_TILE = 256
MIB = 1024 * 1024
VMEM_CAP = 60 * MIB


def _tile(n, pref, mult):
    best = None
    for d in range(mult, min(n, pref) + 1, mult):
        if n % d == 0:
            best = d
    return best if best is not None else n


def _params(sem, vmem_bytes):
    return pltpu.CompilerParams(dimension_semantics=sem,
                                vmem_limit_bytes=int(min(max(vmem_bytes, 16 * MIB), VMEM_CAP)))


def _silu(x):
    return x * jax.nn.sigmoid(x)


def _softplus(x):
    return jnp.maximum(x, 0.0) + jnp.log1p(jnp.exp(-jnp.abs(x)))


def _rms_kernel(x_ref, g_ref, o_ref):
    x = x_ref[...]
    y = x * lax.rsqrt(jnp.mean(x * x, axis=-1, keepdims=True) + EPS)
    o_ref[...] = (y * g_ref[...]).astype(o_ref.dtype)


def rmsnorm(x, g, out_dtype):
    t, d = x.shape
    tr = _tile(t, 512, 16)
    return pl.pallas_call(
        _rms_kernel,
        grid=(t // tr,),
        in_specs=[pl.BlockSpec((tr, d), lambda i: (i, 0)),
                  pl.BlockSpec((1, d), lambda i: (0, 0))],
        out_specs=pl.BlockSpec((tr, d), lambda i: (i, 0)),
        out_shape=jax.ShapeDtypeStruct((t, d), out_dtype),
        compiler_params=_params(("arbitrary",), 6 * tr * d * 4),
        name="rmsnorm",
    )(x, g.reshape(1, d))


def _cast_dot(a_ref, w_ref, wbf_ref, chunks):
    k = wbf_ref.shape[0]
    ck = k // chunks
    lead = (0,) * (len(w_ref.shape) - 2)
    acc = None
    for kc in range(chunks):
        sl = slice(kc * ck, (kc + 1) * ck)
        wbf_ref[sl, :] = w_ref[lead + (sl, slice(None))].astype(BF16)
        part = jnp.dot(a_ref[:, sl], wbf_ref[sl, :], preferred_element_type=F32)
        acc = part if acc is None else acc + part
    return acc


def _mm_kernel(a_ref, w_ref, *rest, n_extra, epilogue, chunks):
    extra = rest[:n_extra]
    o_ref = rest[n_extra]
    wbf_ref = rest[n_extra + 1]

    def finish(acc):
        if epilogue is not None:
            acc = epilogue(acc, *[e[...] for e in extra])
        o_ref[...] = acc.astype(o_ref.dtype)

    @pl.when(pl.program_id(1) == 0)
    def _():
        finish(_cast_dot(a_ref, w_ref, wbf_ref, chunks))

    @pl.when(pl.program_id(1) != 0)
    def _():
        finish(jnp.dot(a_ref[...], wbf_ref[...], preferred_element_type=F32))


def mm(a, w, lead, *, k, ncols, tn, tm, kb_w=0, kb_a=0, col0=0, extras=(), epilogue=None,
       out_dtype=F32, name="mm"):
    m = a.shape[0]
    assert m % tm == 0 and ncols % tn == 0
    nj, ni = ncols // tn, m // tm
    lead = tuple(lead)
    if col0 % tn == 0:
        jb0 = col0 // tn
        w_spec = pl.BlockSpec((None,) * len(lead) + (k, tn), lambda j, i: lead + (kb_w, jb0 + j))
    else:
        w_spec = pl.BlockSpec((pl.Element(1),) * len(lead) + (pl.Element(k), pl.Element(tn)),
                              lambda j, i: lead + (kb_w * k, pl.multiple_of(col0 + j * tn, 128)))
    in_specs = [pl.BlockSpec((tm, k), lambda j, i: (i, kb_a)), w_spec]
    for _, off in extras:
        in_specs.append(pl.BlockSpec((tm, tn), lambda j, i, off=off: (i, off + j)))
    out_bytes = jnp.dtype(out_dtype).itemsize
    vmem = (2 * k * tn * 4 + k * tn * 2 + 2 * tm * k * 2 + 2 * tm * tn * out_bytes
            + sum(2 * tm * tn * e.dtype.itemsize for e, _ in extras) + 3 * tm * tn * 4 + 2 * MIB)
    return pl.pallas_call(
        functools.partial(_mm_kernel, n_extra=len(extras), epilogue=epilogue, chunks=max(1, k // 512)),
        grid=(nj, ni),
        in_specs=in_specs,
        out_specs=pl.BlockSpec((tm, tn), lambda j, i: (i, j)),
        out_shape=jax.ShapeDtypeStruct((m, ncols), out_dtype),
        scratch_shapes=[pltpu.VMEM((k, tn), BF16)],
        compiler_params=_params(("arbitrary", "arbitrary"), vmem),
        name=name,
    )(a, w, *[e for e, _ in extras])


def _shift_rows(x, s):
    row = lax.broadcasted_iota(jnp.int32, x.shape, 0)
    return jnp.where(row >= s, pltpu.roll(x, s, 0), 0.0)


def _sconv_kernel(cb_ref, cc_ref, cx_ref, w_ref, s_ref, v_ref, stp_ref, un_ref, *, nb, seq, ts):
    b = pl.program_id(1)
    w = w_ref[...]

    @pl.when(b < nb)
    def _():
        u = cc_ref[...] * cx_ref[...]
        y = w[0:1] * _shift_rows(u, 2) + w[1:2] * _shift_rows(u, 1) + w[2:3] * u
        v_ref[...] = (cb_ref[...] * y).astype(v_ref.dtype)
        stp_ref[...] = u[seq - 2:seq, :]

    @pl.when(b == nb)
    def _():
        u = cc_ref[0:ts, :] * cx_ref[0:ts, :]
        y = w[0:1] * s_ref[0] + w[1:2] * s_ref[1] + w[2:3] * u
        v_ref[0:ts, :] = (cb_ref[0:ts, :] * y).astype(v_ref.dtype)
        un_ref[...] = u


def sconv(c3, w, state_t, *, nb, seq, ts):
    t = c3.shape[0]
    c = c3.shape[1] // 3
    tc = _tile(c, 256, 128)
    ncb = c // tc
    kern = functools.partial(_sconv_kernel, nb=nb, seq=seq, ts=ts)
    return pl.pallas_call(
        kern,
        grid=(ncb, nb + 1),
        in_specs=[pl.BlockSpec((seq, tc), lambda j, b: (b, j)),
                  pl.BlockSpec((seq, tc), lambda j, b: (b, ncb + j)),
                  pl.BlockSpec((seq, tc), lambda j, b: (b, 2 * ncb + j)),
                  pl.BlockSpec((3, tc), lambda j, b: (0, j)),
                  pl.BlockSpec((2, ts, tc), lambda j, b: (0, 0, j))],
        out_specs=[pl.BlockSpec((seq, tc), lambda j, b: (b, j)),
                   pl.BlockSpec((None, 2, tc), lambda j, b: (jnp.minimum(b, nb - 1), 0, j)),
                   pl.BlockSpec((ts, tc), lambda j, b: (0, j))],
        out_shape=[jax.ShapeDtypeStruct((t, c), BF16),
                   jax.ShapeDtypeStruct((nb, 2, c), F32),
                   jax.ShapeDtypeStruct((ts, c), F32)],
        compiler_params=_params(("arbitrary", "arbitrary"), 16 * seq * tc * 4),
        name="sconv",
    )(c3, c3, c3, w, state_t)


def _xconv_kernel(x_ref, w_ref, bias_ref, s_ref, o_ref, stp_ref, os_ref, *, nb, seq, ts):
    b = pl.program_id(1)
    w = w_ref[...]
    bias = bias_ref[...]

    @pl.when(b < nb)
    def _():
        x = x_ref[...]
        y = (w[0:1] * _shift_rows(x, 3) + w[1:2] * _shift_rows(x, 2) + w[2:3] * _shift_rows(x, 1)
             + w[3:4] * x + bias)
        o_ref[...] = _silu(y).astype(o_ref.dtype)
        stp_ref[...] = x[seq - 3:seq, :]

    @pl.when(b == nb)
    def _():
        x = x_ref[0:ts, :]
        y = w[0:1] * s_ref[0] + w[1:2] * s_ref[1] + w[2:3] * s_ref[2] + w[3:4] * x + bias
        act = _silu(y)
        o_ref[0:ts, :] = act.astype(o_ref.dtype)
        os_ref[...] = act


def xconv(xraw, w, bias, state_t, *, col0, ncols, out_dtype, nb, seq, ts):
    t, ch = xraw.shape
    tc = _tile(ncols, 256, 128)
    assert col0 % tc == 0
    jb0 = col0 // tc
    kern = functools.partial(_xconv_kernel, nb=nb, seq=seq, ts=ts)
    return pl.pallas_call(
        kern,
        grid=(ncols // tc, nb + 1),
        in_specs=[pl.BlockSpec((seq, tc), lambda j, b: (b, jb0 + j)),
                  pl.BlockSpec((4, tc), lambda j, b: (0, jb0 + j)),
                  pl.BlockSpec((1, tc), lambda j, b: (0, jb0 + j)),
                  pl.BlockSpec((3, ts, tc), lambda j, b: (0, 0, jb0 + j))],
        out_specs=[pl.BlockSpec((seq, tc), lambda j, b: (b, j)),
                   pl.BlockSpec((None, 3, tc), lambda j, b: (jnp.minimum(b, nb - 1), 0, j)),
                   pl.BlockSpec((ts, tc), lambda j, b: (0, j))],
        out_shape=[jax.ShapeDtypeStruct((t, ncols), out_dtype),
                   jax.ShapeDtypeStruct((nb, 3, ncols), F32),
                   jax.ShapeDtypeStruct((ts, ncols), F32)],
        compiler_params=_params(("arbitrary", "arbitrary"), 16 * seq * tc * 4),
        name="xconv",
    )(xraw, w, bias.reshape(1, ch), state_t)


def _cumsum_rows(x):
    n = x.shape[0]
    row = lax.broadcasted_iota(jnp.int32, x.shape, 0)
    s = 1
    while s < n:
        x = x + jnp.where(row >= s, pltpu.roll(x, s, 0), 0.0)
        s *= 2
    return x


def _expand_heads(v, onehot):
    v1 = v.astype(BF16)
    r1 = v - v1.astype(F32)
    v2 = r1.astype(BF16)
    v3 = (r1 - v2.astype(F32)).astype(BF16)
    d = functools.partial(jnp.dot, preferred_element_type=F32)
    return d(v1, onehot) + d(v2, onehot) + d(v3, onehot)


def _ssd_prompt_kernel(xs_ref, bm_ref, cm_ref, dt_ref, z_ref, nrm_ref, dtb_ref, alog_ref, dsk_ref,
                       oh_ref, yn_ref, hfin_ref, st_ref, act_ref, dth_ref, ach_ref, *, hpg, nchunk):
    c = pl.program_id(1)
    g = pl.program_id(2)
    q = xs_ref.shape[0]
    p = SSD_HEAD_DIM

    @pl.when(g == 0)
    def _():
        dt_h = _softplus(dt_ref[...] + dtb_ref[...])
        a_h = -jnp.exp(alog_ref[...])
        acum_h = _cumsum_rows(dt_h * a_h)
        dth_ref[...] = dt_h
        ach_ref[...] = acum_h
        act_ref[...] = acum_h.T

    @pl.when(c == 0)
    def _():
        st_ref[g] = jnp.zeros(st_ref.shape[1:], F32)

    onehot = oh_ref[...]
    dt_c = _expand_heads(dth_ref[...], onehot)
    acum = _expand_heads(ach_ref[...], onehot)

    xs = xs_ref[...]
    bm = bm_ref[...]
    cm = cm_ref[...]
    xdt = (xs * dt_c).astype(BF16)
    cb = lax.dot_general(cm, bm, (((1,), (1,)), ((), ())), preferred_element_type=F32)
    li = lax.broadcasted_iota(jnp.int32, (q, q), 0)
    si = lax.broadcasted_iota(jnp.int32, (q, q), 1)
    causal = li >= si
    first_half = si < p

    parts = []
    for pr in range(hpg // 2):
        rhs = xdt[:, pr * 2 * p:(pr + 1) * 2 * p]
        res = []
        for kk in range(2):
            hl = 2 * pr + kk
            col = acum[:, hl * p:hl * p + 1]
            row = act_ref[pl.ds(g * hpg + hl, 1), :]
            dec = jnp.exp(jnp.where(causal, col - row, -jnp.inf))
            res.append(jnp.dot((cb * dec).astype(BF16), rhs, preferred_element_type=F32))
        parts.append(jnp.where(first_half, res[0], res[1]))
    y = jnp.concatenate(parts, axis=1)

    st = st_ref[g]
    y = y + jnp.dot(cm, st.astype(BF16), preferred_element_type=F32) * jnp.exp(acum)
    alast = acum[q - 1:q, :]
    xw = (xs * dt_c * jnp.exp(alast - acum)).astype(BF16)
    bm_t = bm.astype(F32).T.astype(BF16)
    st_new = jnp.exp(alast) * st + jnp.dot(bm_t, xw, preferred_element_type=F32)
    st_ref[g] = st_new

    @pl.when(c == nchunk - 1)
    def _():
        hfin_ref[pl.ds(g * hpg, hpg)] = st_new.T.reshape(hpg, p, st_new.shape[0])

    y = y + dsk_ref[...] * xs
    v = y * _silu(z_ref[...])
    v = v * lax.rsqrt(jnp.mean(v * v, axis=-1, keepdims=True) + EPS)
    yn_ref[...] = (v * nrm_ref[...]).astype(yn_ref.dtype)


def ssd_prompt(xs, bc, dtraw, z, nrm, dtb, alog, dsk_c, onehot, *, nb, seq, t, d_inner, groups, nstate):
    heads = d_inner // SSD_HEAD_DIM
    hpg = heads // groups
    gc = d_inner // groups
    q = SSD_CHUNK if seq % SSD_CHUNK == 0 else seq
    nchunk = seq // q
    assert nstate == 128 and q % 8 == 0 and gc % 128 == 0 and hpg % 2 == 0
    kern = functools.partial(_ssd_prompt_kernel, hpg=hpg, nchunk=nchunk)
    rb = lambda b, c, g: b * nchunk + c
    return pl.pallas_call(
        kern,
        grid=(nb, nchunk, groups),
        in_specs=[pl.BlockSpec((q, gc), lambda b, c, g: (rb(b, c, g), g)),
                  pl.BlockSpec((q, nstate), lambda b, c, g: (rb(b, c, g), g)),
                  pl.BlockSpec((q, nstate), lambda b, c, g: (rb(b, c, g), groups + g)),
                  pl.BlockSpec((q, heads), lambda b, c, g: (rb(b, c, g), 0)),
                  pl.BlockSpec((q, gc), lambda b, c, g: (rb(b, c, g), g)),
                  pl.BlockSpec((1, gc), lambda b, c, g: (0, g)),
                  pl.BlockSpec((1, heads), lambda b, c, g: (0, 0)),
                  pl.BlockSpec((1, heads), lambda b, c, g: (0, 0)),
                  pl.BlockSpec((1, gc), lambda b, c, g: (0, g)),
                  pl.BlockSpec((heads, gc), lambda b, c, g: (0, g))],
        out_specs=[pl.BlockSpec((q, gc), lambda b, c, g: (rb(b, c, g), g)),
                   pl.BlockSpec((None, heads, SSD_HEAD_DIM, nstate), lambda b, c, g: (b, 0, 0, 0))],
        out_shape=[jax.ShapeDtypeStruct((t, d_inner), BF16),
                   jax.ShapeDtypeStruct((nb, heads, SSD_HEAD_DIM, nstate), F32)],
        scratch_shapes=[pltpu.VMEM((groups, nstate, gc), F32), pltpu.VMEM((heads, q), F32),
                        pltpu.VMEM((q, heads), F32), pltpu.VMEM((q, heads), F32)],
        compiler_params=_params(("arbitrary", "arbitrary", "arbitrary"), 40 * MIB),
        name="ssd_prompt",
    )(xs, bc, bc, dtraw, z, nrm.reshape(1, d_inner), dtb.reshape(1, heads), alog.reshape(1, heads),
      dsk_c.reshape(1, d_inner), onehot)


def _ssd_sample_kernel(dtb_ref, alog_ref, dsk_ref, h0_ref, dtt_ref, xst_ref, bm_ref, bmt_ref, cmt_ref,
                       *rest, ts):
    hn_ref, yt_ref = rest[-2:]
    k = pl.program_id(0)
    p = SSD_HEAD_DIM
    dt = _softplus(dtt_ref[pl.ds(k, 1), :] + dtb_ref[k])
    a = -jnp.exp(jnp.full((1, ts), alog_ref[k], F32))
    dec_t = jnp.broadcast_to(jnp.exp(dt * a), (p, ts))
    xs_t = xst_ref[...]
    xdt_t = xs_t * dt
    for j in range(ts):
        hn_ref[j] = h0_ref[j] * dec_t[:, j:j + 1] + xdt_t[:, j:j + 1] * bm_ref[j:j + 1, :]
    h0 = h0_ref[...]
    cm_t = cmt_ref[...]
    res = jnp.dot(h0.reshape(ts * p, h0.shape[2]).astype(BF16), cm_t.astype(BF16),
                  preferred_element_type=F32).reshape(ts, p, ts)
    jj = lax.broadcasted_iota(jnp.int32, res.shape, 0)
    ll = lax.broadcasted_iota(jnp.int32, res.shape, 2)
    y_off = jnp.sum(jnp.where(jj == ll, res, 0.0), axis=0)
    cb = jnp.sum(cm_t * bmt_ref[...], axis=0, keepdims=True)
    yt_ref[...] = cb * xdt_t + y_off * dec_t + dsk_ref[k] * xs_t


def ssd_sample(h0_all, layer, hn_prev, dtraw_t, xs_t, bm, bm_t, cm_t, dtb, alog, dsk, *, hpg):
    depth, ts, rows, nstate = h0_all.shape
    p = SSD_HEAD_DIM
    heads = rows // p
    smem = pl.BlockSpec(memory_space=pltpu.SMEM)
    in_specs = [smem, smem, smem,
                pl.BlockSpec((None, ts, p, nstate), lambda k: (layer, 0, k, 0)),
                pl.BlockSpec((heads, ts), lambda k: (0, 0)),
                pl.BlockSpec((p, ts), lambda k: (k, 0)),
                pl.BlockSpec((None, ts, nstate), lambda k: (k // hpg, 0, 0)),
                pl.BlockSpec((None, nstate, ts), lambda k: (k // hpg, 0, 0)),
                pl.BlockSpec((None, nstate, ts), lambda k: (k // hpg, 0, 0))]
    args = [dtb, alog, dsk, h0_all, dtraw_t, xs_t, bm, bm_t, cm_t]
    aliases = {}
    if hn_prev is not None:
        in_specs.append(pl.BlockSpec(memory_space=pl.ANY))
        args.append(hn_prev)
        aliases = {len(args) - 1: 0}
    return pl.pallas_call(
        functools.partial(_ssd_sample_kernel, ts=ts),
        grid=(heads,),
        in_specs=in_specs,
        out_specs=[pl.BlockSpec((None, ts, p, nstate), lambda k: (layer, 0, k, 0)),
                   pl.BlockSpec((p, ts), lambda k: (k, 0))],
        out_shape=[jax.ShapeDtypeStruct((depth, ts, rows, nstate), F32),
                   jax.ShapeDtypeStruct((rows, ts), F32)],
        input_output_aliases=aliases,
        compiler_params=_params(("arbitrary",), 40 * MIB),
        name="ssd_sample",
    )(*args)


def _gnorm_kernel(y_ref, z_ref, nrm_ref, yn_in_ref, o_ref):
    del yn_in_ref
    v = y_ref[...] * _silu(z_ref[...])
    v = v * lax.rsqrt(jnp.mean(v * v, axis=-1, keepdims=True) + EPS)
    o_ref[...] = (v * nrm_ref[...]).astype(o_ref.dtype)


def gated_norm_sample(y_s, z, nrm, yn, *, tp, groups):
    ts, d_inner = y_s.shape
    gc = d_inner // groups
    assert tp % ts == 0
    rb = tp // ts
    return pl.pallas_call(
        _gnorm_kernel,
        grid=(groups,),
        in_specs=[pl.BlockSpec((ts, gc), lambda g: (0, g)),
                  pl.BlockSpec((ts, gc), lambda g: (rb, g)),
                  pl.BlockSpec((1, gc), lambda g: (0, g)),
                  pl.BlockSpec(memory_space=pl.ANY)],
        out_specs=pl.BlockSpec((ts, gc), lambda g: (rb, g)),
        out_shape=jax.ShapeDtypeStruct(yn.shape, yn.dtype),
        input_output_aliases={3: 0},
        compiler_params=_params(("arbitrary",), 16 * MIB),
        name="gated_norm_sample",
    )(y_s, z, nrm.reshape(1, d_inner), yn)


def _attn_kernel(q_ref, k_ref, v_ref, o_ref, *, nh, hd):
    q = q_ref[...]
    k = k_ref[...].astype(BF16)
    v = v_ref[...].astype(BF16)
    scale = hd ** -0.5
    for h in range(nh):
        sl = slice(h * hd, (h + 1) * hd)
        s = lax.dot_general(q[:, sl], k[:, sl], (((1,), (1,)), ((), ())),
                            preferred_element_type=F32) * scale
        e = jnp.exp(s - jnp.max(s, axis=-1, keepdims=True))
        pr = e / jnp.sum(e, axis=-1, keepdims=True)
        o = jnp.dot(pr.astype(BF16), v[:, sl], preferred_element_type=F32)
        o_ref[:, sl] = o.astype(o_ref.dtype)


def attention(q, k, v, *, nh, hd):
    b, lq, w = q.shape
    m = k.shape[1]
    tq = _tile(lq, 512, 8)
    return pl.pallas_call(
        functools.partial(_attn_kernel, nh=nh, hd=hd),
        grid=(b, lq // tq),
        in_specs=[pl.BlockSpec((None, tq, w), lambda bi, i: (bi, i, 0)),
                  pl.BlockSpec((None, m, w), lambda bi, i: (bi, 0, 0)),
                  pl.BlockSpec((None, m, w), lambda bi, i: (bi, 0, 0))],
        out_specs=pl.BlockSpec((None, tq, w), lambda bi, i: (bi, i, 0)),
        out_shape=jax.ShapeDtypeStruct((b, lq, w), BF16),
        compiler_params=_params(("arbitrary", "arbitrary"), 24 * MIB),
        name="attention",
    )(q, k, v)


def _first_argmax(vals):
    best, arg = vals[0], jnp.zeros(vals[0].shape, jnp.int32)
    for i in range(1, len(vals)):
        better = vals[i] > best
        best = jnp.where(better, vals[i], best)
        arg = jnp.where(better, i, arg)
    return best, arg


def _pick(vals, arg):
    out = vals[0]
    for i in range(1, len(vals)):
        out = jnp.where(arg == i, vals[i], out)
    return out


def _router_kernel(x_ref, g_ref, wt_ref, bias_ref, xn_ref, idx_ref, wts_ref, *, n_experts):
    x = x_ref[...]
    xn = (x * lax.rsqrt(jnp.mean(x * x, axis=-1, keepdims=True) + EPS) * g_ref[...]).astype(BF16)
    xn_ref[...] = xn
    logits = lax.dot_general(wt_ref[...], xn, (((1,), (1,)), ((), ())), preferred_element_type=F32)
    scores = jax.nn.sigmoid(logits)
    biased = scores + bias_ref[...]
    per = n_experts // N_EXPERT_GROUPS
    assert per == 4 and TOP_K == 2
    row = lambda a, e: a[e:e + 1, :]
    gscore = []
    for gi in range(N_EXPERT_GROUPS):
        v = [row(biased, gi * per + i) for i in range(per)]
        a, b = jnp.maximum(v[0], v[1]), jnp.minimum(v[0], v[1])
        c, d = jnp.maximum(v[2], v[3]), jnp.minimum(v[2], v[3])
        gscore.append(jnp.maximum(a, c) + jnp.maximum(jnp.minimum(a, c), jnp.maximum(b, d)))
    _, grp = _first_argmax(gscore)
    bsel = [_pick([row(biased, gi * per + i) for gi in range(N_EXPERT_GROUPS)], grp) for i in range(per)]
    ssel = [_pick([row(scores, gi * per + i) for gi in range(N_EXPERT_GROUPS)], grp) for i in range(per)]
    _, i1 = _first_argmax(bsel)
    _, i2 = _first_argmax([jnp.where(i1 == i, -jnp.inf, bsel[i]) for i in range(per)])
    s1, s2 = _pick(ssel, i1), _pick(ssel, i2)
    tot = s1 + s2
    idx_ref[...] = jnp.concatenate([grp * per + i1, grp * per + i2], axis=0)
    wts_ref[...] = jnp.concatenate([s1 / tot, s2 / tot], axis=0)


def router(x, g, w_router, router_bias):
    t, d = x.shape
    e = w_router.shape[1]
    tr = _tile(t, 640, 128)
    return pl.pallas_call(
        functools.partial(_router_kernel, n_experts=e),
        grid=(t // tr,),
        in_specs=[pl.BlockSpec((tr, d), lambda i: (i, 0)),
                  pl.BlockSpec((1, d), lambda i: (0, 0)),
                  pl.BlockSpec((e, d), lambda i: (0, 0)),
                  pl.BlockSpec((e, 1), lambda i: (0, 0))],
        out_specs=[pl.BlockSpec((tr, d), lambda i: (i, 0)),
                   pl.BlockSpec((TOP_K, tr), lambda i: (0, i)),
                   pl.BlockSpec((TOP_K, tr), lambda i: (0, i))],
        out_shape=[jax.ShapeDtypeStruct((t, d), BF16),
                   jax.ShapeDtypeStruct((TOP_K, t), jnp.int32),
                   jax.ShapeDtypeStruct((TOP_K, t), F32)],
        compiler_params=_params(("arbitrary",), 8 * tr * d * 4 + 8 * MIB),
        name="router",
    )(x, g.reshape(1, d), w_router.T.astype(BF16), router_bias.astype(F32).reshape(e, 1))


def _moe_up_kernel(te_ref, tf_ref, nt_ref, x_ref, rw_ref, wg_ref, wu_ref, h_ref, wgb_ref, wub_ref):
    t = pl.program_id(1)

    def finish(gate, up):
        h_ref[...] = (_silu(gate) * up * rw_ref[...]).astype(h_ref.dtype)

    active = t < nt_ref[0]
    first = tf_ref[t] == 1
    chunks = max(1, wgb_ref.shape[0] // 512)

    @pl.when(active & first)
    def _():
        finish(_cast_dot(x_ref, wg_ref, wgb_ref, chunks), _cast_dot(x_ref, wu_ref, wub_ref, chunks))

    @pl.when(active & jnp.logical_not(first))
    def _():
        x = x_ref[...]
        finish(jnp.dot(x, wgb_ref[...], preferred_element_type=F32),
               jnp.dot(x, wub_ref[...], preferred_element_type=F32))

    @pl.when(jnp.logical_not(active))
    def _():
        h_ref[...] = jnp.zeros_like(h_ref)


def moe_up(x_sorted, row_w, w_gate, w_up, layer, tile_e, tile_first, n_tiles):
    pr, d = x_sorted.shape
    f = w_gate.shape[-1]
    tf = _tile(f, 512, 128)
    ntile = pr // MOE_TILE

    def tix(t, nt):
        return jnp.minimum(t, nt[0] - 1)

    grid_spec = pltpu.PrefetchScalarGridSpec(
        num_scalar_prefetch=3,
        grid=(f // tf, ntile),
        in_specs=[pl.BlockSpec((MOE_TILE, d), lambda fb, t, te, tfi, nt: (tix(t, nt), 0)),
                  pl.BlockSpec((MOE_TILE, 1), lambda fb, t, te, tfi, nt: (tix(t, nt), 0)),
                  pl.BlockSpec((None, None, d, tf), lambda fb, t, te, tfi, nt: (layer, te[tix(t, nt)], 0, fb)),
                  pl.BlockSpec((None, None, d, tf), lambda fb, t, te, tfi, nt: (layer, te[tix(t, nt)], 0, fb))],
        out_specs=pl.BlockSpec((MOE_TILE, tf), lambda fb, t, te, tfi, nt: (t, fb)),
        scratch_shapes=[pltpu.VMEM((d, tf), BF16), pltpu.VMEM((d, tf), BF16)],
    )
    vmem = 4 * d * tf * 4 + 2 * d * tf * 2 + 2 * MOE_TILE * d * 2 + 8 * MOE_TILE * tf * 4 + 2 * MIB
    return pl.pallas_call(
        _moe_up_kernel,
        grid_spec=grid_spec,
        out_shape=jax.ShapeDtypeStruct((pr, f), BF16),
        compiler_params=_params(("arbitrary", "arbitrary"), vmem),
        name="moe_up",
    )(tile_e, tile_first, n_tiles, x_sorted, row_w, w_gate, w_up)


def _moe_down_kernel(te_ref, tf_ref, nt_ref, h_ref, wd_ref, o_ref, wdb_ref):
    t = pl.program_id(1)

    active = t < nt_ref[0]
    first = tf_ref[t] == 1

    @pl.when(active & first)
    def _():
        o_ref[...] = _cast_dot(h_ref, wd_ref, wdb_ref, max(1, wdb_ref.shape[0] // 256))

    @pl.when(active & jnp.logical_not(first))
    def _():
        o_ref[...] = jnp.dot(h_ref[...], wdb_ref[...], preferred_element_type=F32)

    @pl.when(jnp.logical_not(active))
    def _():
        o_ref[...] = jnp.zeros_like(o_ref)


def moe_down(h_sorted, w_down, layer, tile_e, tile_first, n_tiles):
    pr, f = h_sorted.shape
    d = w_down.shape[-1]
    tn = _tile(d, 2048, 128)
    ntile = pr // MOE_TILE

    def tix(t, nt):
        return jnp.minimum(t, nt[0] - 1)

    grid_spec = pltpu.PrefetchScalarGridSpec(
        num_scalar_prefetch=3,
        grid=(d // tn, ntile),
        in_specs=[pl.BlockSpec((MOE_TILE, f), lambda nb, t, te, tfi, nt: (tix(t, nt), 0)),
                  pl.BlockSpec((None, None, f, tn), lambda nb, t, te, tfi, nt: (layer, te[tix(t, nt)], 0, nb))],
        out_specs=pl.BlockSpec((MOE_TILE, tn), lambda nb, t, te, tfi, nt: (t, nb)),
        scratch_shapes=[pltpu.VMEM((f, tn), BF16)],
    )
    vmem = 2 * f * tn * 4 + f * tn * 2 + 2 * MOE_TILE * f * 2 + 4 * MOE_TILE * tn * 4 + 2 * MIB
    return pl.pallas_call(
        _moe_down_kernel,
        grid_spec=grid_spec,
        out_shape=jax.ShapeDtypeStruct((pr, d), F32),
        compiler_params=_params(("arbitrary", "arbitrary"), vmem),
        name="moe_down",
    )(tile_e, tile_first, n_tiles, h_sorted, w_down)


def _dispatch_kernel(e_ref, pos_ref, tend_ref, cnt_ref, run_ref, tstart_ref, *, rows):
    ph = pl.program_id(0)
    i = pl.program_id(1)
    lanes = cnt_ref.shape[1]
    onehot = (e_ref[...] == lax.broadcasted_iota(jnp.int32, (rows, lanes), 1)).astype(F32)
    colsum = jnp.sum(onehot, axis=0, keepdims=True)

    @pl.when((ph == 0) & (i == 0))
    def _():
        cnt_ref[...] = jnp.zeros_like(cnt_ref)
        run_ref[...] = jnp.zeros_like(run_ref)

    @pl.when(ph == 0)
    def _():
        cnt_ref[...] += colsum

    @pl.when((ph == 1) & (i == 0))
    def _():
        tiles = jnp.floor((cnt_ref[...] + (MOE_TILE - 1)) * (1.0 / MOE_TILE))
        r = lax.broadcasted_iota(jnp.int32, (lanes, lanes), 0)
        c = lax.broadcasted_iota(jnp.int32, (lanes, lanes), 1)
        before = (r < c).astype(BF16)
        start = jnp.dot(jnp.broadcast_to(tiles, (8, lanes)).astype(BF16), before,
                        preferred_element_type=F32)[0:1]
        tstart_ref[...] = start
        tend_ref[...] = (start + tiles).astype(jnp.int32)

    @pl.when(ph == 1)
    def _():
        r = lax.broadcasted_iota(jnp.int32, (rows, rows), 0)
        c = lax.broadcasted_iota(jnp.int32, (rows, rows), 1)
        earlier = (c < r).astype(BF16)
        rank = jnp.dot(earlier, onehot.astype(BF16), preferred_element_type=F32) + run_ref[...]
        row = tstart_ref[...] * MOE_TILE + rank
        pos_ref[...] = jnp.sum(onehot * row, axis=1, keepdims=True).astype(jnp.int32)
        run_ref[...] += colsum


def dispatch(e_col, n_experts):
    na = e_col.shape[0]
    lanes = -(-n_experts // 128) * 128
    rows = _tile(na, 640, 128)
    assert rows % 128 == 0 and na + n_experts * MOE_TILE < 2 ** 24 and na // MOE_TILE + n_experts <= 256
    nblk = na // rows
    return pl.pallas_call(
        functools.partial(_dispatch_kernel, rows=rows),
        grid=(2, nblk),
        in_specs=[pl.BlockSpec((rows, 1), lambda ph, i: (i, 0))],
        out_specs=[pl.BlockSpec((rows, 1), lambda ph, i: (ph * i, 0)),
                   pl.BlockSpec((1, lanes), lambda ph, i: (0, 0))],
        out_shape=[jax.ShapeDtypeStruct((na, 1), jnp.int32),
                   jax.ShapeDtypeStruct((1, lanes), jnp.int32)],
        scratch_shapes=[pltpu.VMEM((1, lanes), F32), pltpu.VMEM((1, lanes), F32),
                        pltpu.VMEM((1, lanes), F32)],
        compiler_params=_params(("arbitrary", "arbitrary"), 16 * MIB),
        name="dispatch",
    )(e_col)


def moe(x, xn, idx_t, wts_t, w_gate, w_up, w_down, layer):
    t, d = x.shape
    n_experts = w_gate.shape[1]
    na = t * TOP_K
    ntile = -(-na // MOE_TILE) + n_experts
    pos_col, tile_end = dispatch(idx_t.reshape(na, 1), n_experts)
    pos = pos_col.reshape(TOP_K, t)
    tile_end = tile_end[0, :n_experts]
    tok = jnp.tile(jnp.arange(t, dtype=jnp.int32), TOP_K)
    row_token = jnp.zeros((ntile * MOE_TILE,), jnp.int32).at[pos_col[:, 0]].set(tok)
    row_w = jnp.zeros((ntile * MOE_TILE, 1), F32).at[pos_col[:, 0], 0].set(wts_t.reshape(-1))
    n_tiles = tile_end[-1:]
    tile_ids = jnp.arange(ntile, dtype=jnp.int32)
    tile_e = jnp.minimum(jnp.searchsorted(tile_end, tile_ids, side="right"), n_experts - 1).astype(jnp.int32)
    tile_first = jnp.concatenate([jnp.ones((1,), jnp.int32),
                                  (tile_e[1:] != tile_e[:-1]).astype(jnp.int32)])

    gather = lambda a, ix: a.at[ix].get(mode="promise_in_bounds")
    x_sorted = gather(xn, row_token)
    h_sorted = moe_up(x_sorted, row_w, w_gate, w_up, layer, tile_e, tile_first, n_tiles)
    o_sorted = moe_down(h_sorted, w_down, layer, tile_e, tile_first, n_tiles)
    y = gather(o_sorted, pos[0])
    for kk in range(1, TOP_K):
        y = y + gather(o_sorted, pos[kk])
    return x + y


def kernel(x_prompt, x_sample, mem_prompt, cache_mem_k, cache_mem_v, state_sconv, state_ssm_conv, state_ssm,
           norm_mix, w_in, sconv_w, w_sconv_out, ssd_conv_w, ssd_conv_b, ssd_dt_bias, ssd_a_log, ssd_d, ssd_norm,
           w_ssd_out, w_mix_out, norm_xattn, norm_mem, w_xk, w_xv, w_xq, w_xo, norm_moe, w_router, router_bias,
           w_gate, w_up, w_down, norm_final):
    nb, seq, d = x_prompt.shape
    ts = x_sample.shape[0]
    depth = w_in.shape[0]
    tp = nb * seq
    t = tp + ts
    c = sconv_w.shape[-1]
    d_inner = ssd_norm.shape[-1]
    heads = ssd_dt_bias.shape[-1]
    nstate = state_ssm.shape[-1]
    ch = ssd_conv_w.shape[-1]
    groups = (ch - d_inner) // (2 * nstate)
    hpg = heads // groups
    n_mem, nh, hd = cache_mem_k.shape[2:]
    xw = nh * hd
    assert x_sample.shape[1] == 1 and c == d and d_inner == 2 * d and heads * SSD_HEAD_DIM == d_inner
    assert ts <= seq and tp % ts == 0 and heads % 128 == 0

    col_z = 3 * c
    col_xbc = col_z + d_inner
    col_dt = col_xbc + ch
    col_g = col_dt + heads

    tm = _tile(t, 1040, 16)
    tn = 512
    mmk = functools.partial(mm, k=d, tm=tm)
    mmb = functools.partial(mm, k=d, tm=_tile(t, 1664, 16))
    sig = lambda acc: jax.nn.sigmoid(acc)

    onehot = (jnp.arange(heads)[:, None] == (jnp.arange(d_inner) // SSD_HEAD_DIM)[None, :]).astype(BF16)
    mem2 = mem_prompt.reshape(nb * n_mem, d)

    x = jnp.concatenate([x_prompt.reshape(tp, d), x_sample.reshape(ts, d)], axis=0)
    outs = {n: [] for n in ("mk", "mv", "scp", "ccp", "hp", "scs", "ccs")}
    hs = None

    for l in range(depth):
        xn = rmsnorm(x, norm_mix[l], BF16)
        c3 = mmb(xn, w_in, (l,), col0=0, ncols=3 * c, tn=tn, name="in_conv")
        z = mmb(xn, w_in, (l,), col0=col_z, ncols=d_inner, tn=tn, name="in_z")
        xraw = mmb(xn, w_in, (l,), col0=col_xbc, ncols=ch, tn=tn, name="in_xbc")
        dtraw = mmk(xn, w_in, (l,), col0=col_dt, ncols=heads, tn=heads, name="in_dt")
        gates = mmb(xn, w_in, (l,), col0=col_g, ncols=2 * d, tn=tn, epilogue=sig, name="in_gates")

        sc_state = jnp.moveaxis(state_sconv[l], 1, 0)
        v, scp, u_new = sconv(c3, sconv_w[l], sc_state, nb=nb, seq=seq, ts=ts)
        cc_state = jnp.moveaxis(state_ssm_conv[l], 1, 0)
        cargs = dict(nb=nb, seq=seq, ts=ts)
        xs, ccp_x, xs_s = xconv(xraw, ssd_conv_w[l], ssd_conv_b[l], cc_state, col0=0, ncols=d_inner,
                                out_dtype=F32, **cargs)
        bc, ccp_bc, bc_s = xconv(xraw, ssd_conv_w[l], ssd_conv_b[l], cc_state, col0=d_inner,
                                 ncols=ch - d_inner, out_dtype=BF16, **cargs)
        ccp = jnp.concatenate([ccp_x, ccp_bc], axis=-1)

        dsk_c = jnp.repeat(ssd_d[l], SSD_HEAD_DIM)
        yn, hp = ssd_prompt(xs, bc, dtraw, z, ssd_norm[l], ssd_dt_bias[l], ssd_a_log[l], dsk_c, onehot,
                            nb=nb, seq=seq, t=t, d_inner=d_inner, groups=groups, nstate=nstate)

        bm_s = bc_s[:, :groups * nstate].reshape(ts, groups, nstate)
        cm_s = bc_s[:, groups * nstate:].reshape(ts, groups, nstate)
        hs, y_t = ssd_sample(state_ssm.reshape(depth, ts, d_inner, nstate), l, hs, dtraw[tp:].T, xs_s.T,
                             jnp.moveaxis(bm_s, 1, 0), jnp.transpose(bm_s, (1, 2, 0)),
                             jnp.transpose(cm_s, (1, 2, 0)),
                             ssd_dt_bias[l], ssd_a_log[l], ssd_d[l], hpg=hpg)
        yn = gated_norm_sample(y_t.T, z, ssd_norm[l], yn, tp=tp, groups=groups)

        y_conv = mmk(v, w_sconv_out, (l,), ncols=d, tn=tn, extras=[(gates, 0)],
                     epilogue=lambda acc, gc_: acc * gc_, name="sconv_out")
        y_ssd0 = mmb(yn, w_ssd_out, (l,), ncols=d, tn=tn, kb_a=0, kb_w=0, name="ssd_out0")
        mixed = mmk(yn, w_ssd_out, (l,), ncols=d, tn=tn, kb_a=1, kb_w=1,
                    extras=[(y_ssd0, 0), (y_conv, 0), (gates, d // tn)],
                    epilogue=lambda acc, y0, yc, gs: yc + gs * (y0 + acc), out_dtype=BF16, name="ssd_out1")
        x = mmk(mixed, w_mix_out, (l,), ncols=d, tn=tn, extras=[(x, 0)],
                epilogue=lambda acc, xr: xr + acc, name="mix_out")

        memn = rmsnorm(mem2, norm_mem[l], BF16)
        tmm = _tile(nb * n_mem, 1024, 16)
        k_p = mm(memn, w_xk, (l,), k=d, ncols=xw, tn=tn, tm=tmm, name="mem_k")
        v_p = mm(memn, w_xv, (l,), k=d, ncols=xw, tn=tn, tm=tmm, name="mem_v")
        xn = rmsnorm(x, norm_xattn[l], BF16)
        qa = mmk(xn, w_xq, (l,), ncols=xw, tn=tn, out_dtype=BF16, name="xq")
        o_p = attention(qa[:tp].reshape(nb, seq, xw), k_p.reshape(nb, n_mem, xw), v_p.reshape(nb, n_mem, xw),
                        nh=nh, hd=hd)
        q_s = jnp.broadcast_to(qa[tp:].reshape(ts, 1, xw), (ts, 8, xw))
        o_s = attention(q_s, cache_mem_k[l].reshape(ts, n_mem, xw), cache_mem_v[l].reshape(ts, n_mem, xw),
                        nh=nh, hd=hd)
        o = jnp.concatenate([o_p.reshape(tp, xw), o_s[:, 0]], axis=0)
        x = mm(o, w_xo, (l,), k=xw, ncols=d, tn=1024, tm=tm, extras=[(x, 0)],
               epilogue=lambda acc, xr: xr + acc, name="xo")

        xn, idx_t, wts_t = router(x, norm_moe[l], w_router, router_bias)
        x = moe(x, xn, idx_t, wts_t, w_gate, w_up, w_down, l)

        outs["mk"].append(k_p.reshape(nb, n_mem, nh, hd))
        outs["mv"].append(v_p.reshape(nb, n_mem, nh, hd))
        outs["scp"].append(scp)
        outs["ccp"].append(ccp)
        outs["hp"].append(hp)
        outs["scs"].append(jnp.stack([state_sconv[l][:, 1], u_new], axis=1))
        outs["ccs"].append(jnp.concatenate([state_ssm_conv[l][:, 1:], xraw[tp:, None, :]], axis=1))

    y = rmsnorm(x, norm_final, F32)
    st = lambda n: jnp.stack(outs[n])
    return (y[:tp].reshape(nb, seq, d), y[tp:].reshape(ts, 1, d), st("mk"), st("mv"), st("scp"), st("ccp"),
            st("hp"), st("scs"), st("ccs"), hs.reshape(depth, ts, heads, SSD_HEAD_DIM, nstate))
```

```python
import functools

import jax
import jax.numpy as jnp
from jax import lax
from jax.experimental import pallas as pl
from jax.experimental.pallas import tpu as pltpu

F32 = jnp.float32
BF16 = jnp.bfloat16
EPS = 1e-6
SSD_HEAD_DIM = 64
SSD_CHUNK = 128
N_EXPERT_GROUPS = 8
TOP_K = 2
MOE_TILE = 256
MIB = 1024 * 1024
VMEM_CAP = 60 * MIB


def _tile(n, pref, mult):
    best = None
    for d in range(mult, min(n, pref) + 1, mult):
        if n % d == 0:
            best = d
    return best if best is not None else n


def _params(sem, vmem_bytes):
    return pltpu.CompilerParams(dimension_semantics=sem,
                                vmem_limit_bytes=int(min(max(vmem_bytes, 16 * MIB), VMEM_CAP)))


def _silu(x):
    return x * jax.nn.sigmoid(x)


def _softplus(x):
    return jnp.maximum(x, 0.0) + jnp.log1p(jnp.exp(-jnp.abs(x)))


def _rms_kernel(x_ref, g_ref, o_ref):
    x = x_ref[...]
    y = x * lax.rsqrt(jnp.mean(x * x, axis=-1, keepdims=True) + EPS)
    o_ref[...] = (y * g_ref[...]).astype(o_ref.dtype)


def rmsnorm(x, g, out_dtype, row0=0, nrows=None):
    d = x.shape[1]
    t = x.shape[0] if nrows is None else nrows
    tr = _tile(t, 512, 16)
    assert row0 % tr == 0
    rb0 = row0 // tr
    return pl.pallas_call(
        _rms_kernel,
        grid=(t // tr,),
        in_specs=[pl.BlockSpec((tr, d), lambda i: (rb0 + i, 0)),
                  pl.BlockSpec((1, d), lambda i: (0, 0))],
        out_specs=pl.BlockSpec((tr, d), lambda i: (i, 0)),
        out_shape=jax.ShapeDtypeStruct((t, d), out_dtype),
        compiler_params=_params(("arbitrary",), 6 * tr * d * 4),
        name="rmsnorm",
    )(x, g.reshape(1, d))


def _cast_dot(a_ref, w_ref, wbf_ref, chunks):
    k = wbf_ref.shape[0]
    ck = k // chunks
    lead = (0,) * (len(w_ref.shape) - 2)
    acc = None
    for kc in range(chunks):
        sl = slice(kc * ck, (kc + 1) * ck)
        wbf_ref[sl, :] = w_ref[lead + (sl, slice(None))].astype(BF16)
        part = jnp.dot(a_ref[:, sl], wbf_ref[sl, :], preferred_element_type=F32)
        acc = part if acc is None else acc + part
    return acc


def _mm_kernel(a_ref, w_ref, *rest, n_extra, epilogue, chunks):
    extra = rest[:n_extra]
    o_ref = rest[n_extra]
    wbf_ref = rest[n_extra + 1]

    def finish(acc):
        if epilogue is not None:
            acc = epilogue(acc, *[e[...] for e in extra])
        o_ref[...] = acc.astype(o_ref.dtype)

    @pl.when(pl.program_id(1) == 0)
    def _():
        finish(_cast_dot(a_ref, w_ref, wbf_ref, chunks))

    @pl.when(pl.program_id(1) != 0)
    def _():
        finish(jnp.dot(a_ref[...], wbf_ref[...], preferred_element_type=F32))


def mm(a, w, lead, *, k, ncols, tn, tm, kb_w=0, kb_a=0, col0=0, extras=(), epilogue=None,
       out_dtype=F32, name="mm"):
    m = a.shape[0]
    assert m % tm == 0 and ncols % tn == 0
    nj, ni = ncols // tn, m // tm
    lead = tuple(lead)
    if col0 % tn == 0:
        jb0 = col0 // tn
        w_spec = pl.BlockSpec((None,) * len(lead) + (k, tn), lambda j, i: lead + (kb_w, jb0 + j))
    else:
        w_spec = pl.BlockSpec((pl.Element(1),) * len(lead) + (pl.Element(k), pl.Element(tn)),
                              lambda j, i: lead + (kb_w * k, pl.multiple_of(col0 + j * tn, 128)))
    in_specs = [pl.BlockSpec((tm, k), lambda j, i: (i, kb_a)), w_spec]
    for _, off in extras:
        in_specs.append(pl.BlockSpec((tm, tn), lambda j, i, off=off: (i, off + j)))
    out_bytes = jnp.dtype(out_dtype).itemsize
    vmem = (2 * k * tn * 4 + k * tn * 2 + 2 * tm * k * 2 + 2 * tm * tn * out_bytes
            + sum(2 * tm * tn * e.dtype.itemsize for e, _ in extras) + 3 * tm * tn * 4 + 2 * MIB)
    return pl.pallas_call(
        functools.partial(_mm_kernel, n_extra=len(extras), epilogue=epilogue, chunks=max(1, k // 512)),
        grid=(nj, ni),
        in_specs=in_specs,
        out_specs=pl.BlockSpec((tm, tn), lambda j, i: (i, j)),
        out_shape=jax.ShapeDtypeStruct((m, ncols), out_dtype),
        scratch_shapes=[pltpu.VMEM((k, tn), BF16)],
        compiler_params=_params(("arbitrary", "arbitrary"), vmem),
        name=name,
    )(a, w, *[e for e, _ in extras])


def _shift_rows(x, s):
    row = lax.broadcasted_iota(jnp.int32, x.shape, 0)
    return jnp.where(row >= s, pltpu.roll(x, s, 0), 0.0)


def _sconv_kernel(cb_ref, cc_ref, cx_ref, w_ref, s_ref, v_ref, stp_ref, un_ref, *, nb, seq, ts):
    b = pl.program_id(1)
    w = w_ref[...]

    @pl.when(b < nb)
    def _():
        u = cc_ref[...] * cx_ref[...]
        y = w[0:1] * _shift_rows(u, 2) + w[1:2] * _shift_rows(u, 1) + w[2:3] * u
        v_ref[...] = (cb_ref[...] * y).astype(v_ref.dtype)
        stp_ref[...] = u[seq - 2:seq, :]

    @pl.when(b == nb)
    def _():
        u = cc_ref[0:ts, :] * cx_ref[0:ts, :]
        y = w[0:1] * s_ref[0] + w[1:2] * s_ref[1] + w[2:3] * u
        v_ref[0:ts, :] = (cb_ref[0:ts, :] * y).astype(v_ref.dtype)
        un_ref[...] = u


def sconv(c3, w, state_t, *, nb, seq, ts):
    t = c3.shape[0]
    c = c3.shape[1] // 3
    tc = _tile(c, 256, 128)
    ncb = c // tc
    kern = functools.partial(_sconv_kernel, nb=nb, seq=seq, ts=ts)
    return pl.pallas_call(
        kern,
        grid=(ncb, nb + 1),
        in_specs=[pl.BlockSpec((seq, tc), lambda j, b: (b, j)),
                  pl.BlockSpec((seq, tc), lambda j, b: (b, ncb + j)),
                  pl.BlockSpec((seq, tc), lambda j, b: (b, 2 * ncb + j)),
                  pl.BlockSpec((3, tc), lambda j, b: (0, j)),
                  pl.BlockSpec((2, ts, tc), lambda j, b: (0, 0, j))],
        out_specs=[pl.BlockSpec((seq, tc), lambda j, b: (b, j)),
                   pl.BlockSpec((None, 2, tc), lambda j, b: (jnp.minimum(b, nb - 1), 0, j)),
                   pl.BlockSpec((ts, tc), lambda j, b: (0, j))],
        out_shape=[jax.ShapeDtypeStruct((t, c), BF16),
                   jax.ShapeDtypeStruct((nb, 2, c), F32),
                   jax.ShapeDtypeStruct((ts, c), F32)],
        compiler_params=_params(("arbitrary", "arbitrary"), 16 * seq * tc * 4),
        name="sconv",
    )(c3, c3, c3, w, state_t)


def _xconv_kernel(x_ref, w_ref, bias_ref, s_ref, o_ref, stp_ref, os_ref, *, nb, seq, ts):
    b = pl.program_id(1)
    w = w_ref[...]
    bias = bias_ref[...]

    @pl.when(b < nb)
    def _():
        x = x_ref[...]
        y = (w[0:1] * _shift_rows(x, 3) + w[1:2] * _shift_rows(x, 2) + w[2:3] * _shift_rows(x, 1)
             + w[3:4] * x + bias)
        o_ref[...] = _silu(y).astype(o_ref.dtype)
        stp_ref[...] = x[seq - 3:seq, :]

    @pl.when(b == nb)
    def _():
        x = x_ref[0:ts, :]
        y = w[0:1] * s_ref[0] + w[1:2] * s_ref[1] + w[2:3] * s_ref[2] + w[3:4] * x + bias
        act = _silu(y)
        o_ref[0:ts, :] = act.astype(o_ref.dtype)
        os_ref[...] = act


def xconv(xraw, w, bias, state_t, *, col0, ncols, out_dtype, nb, seq, ts):
    t, ch = xraw.shape
    tc = _tile(ncols, 256, 128)
    assert col0 % tc == 0
    jb0 = col0 // tc
    kern = functools.partial(_xconv_kernel, nb=nb, seq=seq, ts=ts)
    return pl.pallas_call(
        kern,
        grid=(ncols // tc, nb + 1),
        in_specs=[pl.BlockSpec((seq, tc), lambda j, b: (b, jb0 + j)),
                  pl.BlockSpec((4, tc), lambda j, b: (0, jb0 + j)),
                  pl.BlockSpec((1, tc), lambda j, b: (0, jb0 + j)),
                  pl.BlockSpec((3, ts, tc), lambda j, b: (0, 0, jb0 + j))],
        out_specs=[pl.BlockSpec((seq, tc), lambda j, b: (b, j)),
                   pl.BlockSpec((None, 3, tc), lambda j, b: (jnp.minimum(b, nb - 1), 0, j)),
                   pl.BlockSpec((ts, tc), lambda j, b: (0, j))],
        out_shape=[jax.ShapeDtypeStruct((t, ncols), out_dtype),
                   jax.ShapeDtypeStruct((nb, 3, ncols), F32),
                   jax.ShapeDtypeStruct((ts, ncols), F32)],
        compiler_params=_params(("arbitrary", "arbitrary"), 16 * seq * tc * 4),
        name="xconv",
    )(xraw, w, bias.reshape(1, ch), state_t)


def _cumsum_rows(x):
    n = x.shape[0]
    row = lax.broadcasted_iota(jnp.int32, x.shape, 0)
    s = 1
    while s < n:
        x = x + jnp.where(row >= s, pltpu.roll(x, s, 0), 0.0)
        s *= 2
    return x


def _expand_heads(v, onehot):
    v1 = v.astype(BF16)
    r1 = v - v1.astype(F32)
    v2 = r1.astype(BF16)
    v3 = (r1 - v2.astype(F32)).astype(BF16)
    d = functools.partial(jnp.dot, preferred_element_type=F32)
    return d(v1, onehot) + d(v2, onehot) + d(v3, onehot)


def _ssd_prompt_kernel(xs_ref, bm_ref, cm_ref, dt_ref, z_ref, nrm_ref, dtb_ref, alog_ref, dsk_ref,
                       oh_ref, yn_ref, hfin_ref, st_ref, act_ref, dth_ref, ach_ref, *, hpg, nchunk):
    c = pl.program_id(1)
    g = pl.program_id(2)
    q = xs_ref.shape[0]
    p = SSD_HEAD_DIM

    @pl.when(g == 0)
    def _():
        dt_h = _softplus(dt_ref[...] + dtb_ref[...])
        a_h = -jnp.exp(alog_ref[...])
        acum_h = _cumsum_rows(dt_h * a_h)
        dth_ref[...] = dt_h
        ach_ref[...] = acum_h
        act_ref[...] = acum_h.T

    @pl.when(c == 0)
    def _():
        st_ref[g] = jnp.zeros(st_ref.shape[1:], F32)

    onehot = oh_ref[...]
    dt_c = _expand_heads(dth_ref[...], onehot)
    acum = _expand_heads(ach_ref[...], onehot)

    xs = xs_ref[...]
    bm = bm_ref[...]
    cm = cm_ref[...]
    xdt = (xs * dt_c).astype(BF16)
    cb = lax.dot_general(cm, bm, (((1,), (1,)), ((), ())), preferred_element_type=F32)
    li = lax.broadcasted_iota(jnp.int32, (q, q), 0)
    si = lax.broadcasted_iota(jnp.int32, (q, q), 1)
    causal = li >= si
    first_half = si < p

    parts = []
    for pr in range(hpg // 2):
        rhs = xdt[:, pr * 2 * p:(pr + 1) * 2 * p]
        res = []
        for kk in range(2):
            hl = 2 * pr + kk
            col = acum[:, hl * p:hl * p + 1]
            row = act_ref[pl.ds(g * hpg + hl, 1), :]
            dec = jnp.exp(jnp.where(causal, col - row, -jnp.inf))
            res.append(jnp.dot((cb * dec).astype(BF16), rhs, preferred_element_type=F32))
        parts.append(jnp.where(first_half, res[0], res[1]))
    y = jnp.concatenate(parts, axis=1)

    st = st_ref[g]
    y = y + jnp.dot(cm, st.astype(BF16), preferred_element_type=F32) * jnp.exp(acum)
    alast = acum[q - 1:q, :]
    xw = (xs * dt_c * jnp.exp(alast - acum)).astype(BF16)
    bm_t = bm.astype(F32).T.astype(BF16)
    st_new = jnp.exp(alast) * st + jnp.dot(bm_t, xw, preferred_element_type=F32)
    st_ref[g] = st_new

    @pl.when(c == nchunk - 1)
    def _():
        hfin_ref[pl.ds(g * hpg, hpg)] = st_new.T.reshape(hpg, p, st_new.shape[0])

    y = y + dsk_ref[...] * xs
    v = y * _silu(z_ref[...])
    v = v * lax.rsqrt(jnp.mean(v * v, axis=-1, keepdims=True) + EPS)
    yn_ref[...] = (v * nrm_ref[...]).astype(yn_ref.dtype)


def ssd_prompt(xs, bc, dtraw, z, nrm, dtb, alog, dsk_c, onehot, *, nb, seq, t, d_inner, groups, nstate):
    heads = d_inner // SSD_HEAD_DIM
    hpg = heads // groups
    gc = d_inner // groups
    q = SSD_CHUNK if seq % SSD_CHUNK == 0 else seq
    nchunk = seq // q
    assert nstate == 128 and q % 8 == 0 and gc % 128 == 0 and hpg % 2 == 0
    kern = functools.partial(_ssd_prompt_kernel, hpg=hpg, nchunk=nchunk)
    rb = lambda b, c, g: b * nchunk + c
    return pl.pallas_call(
        kern,
        grid=(nb, nchunk, groups),
        in_specs=[pl.BlockSpec((q, gc), lambda b, c, g: (rb(b, c, g), g)),
                  pl.BlockSpec((q, nstate), lambda b, c, g: (rb(b, c, g), g)),
                  pl.BlockSpec((q, nstate), lambda b, c, g: (rb(b, c, g), groups + g)),
                  pl.BlockSpec((q, heads), lambda b, c, g: (rb(b, c, g), 0)),
                  pl.BlockSpec((q, gc), lambda b, c, g: (rb(b, c, g), g)),
                  pl.BlockSpec((1, gc), lambda b, c, g: (0, g)),
                  pl.BlockSpec((1, heads), lambda b, c, g: (0, 0)),
                  pl.BlockSpec((1, heads), lambda b, c, g: (0, 0)),
                  pl.BlockSpec((1, gc), lambda b, c, g: (0, g)),
                  pl.BlockSpec((heads, gc), lambda b, c, g: (0, g))],
        out_specs=[pl.BlockSpec((q, gc), lambda b, c, g: (rb(b, c, g), g)),
                   pl.BlockSpec((None, heads, SSD_HEAD_DIM, nstate), lambda b, c, g: (b, 0, 0, 0))],
        out_shape=[jax.ShapeDtypeStruct((t, d_inner), BF16),
                   jax.ShapeDtypeStruct((nb, heads, SSD_HEAD_DIM, nstate), F32)],
        scratch_shapes=[pltpu.VMEM((groups, nstate, gc), F32), pltpu.VMEM((heads, q), F32),
                        pltpu.VMEM((q, heads), F32), pltpu.VMEM((q, heads), F32)],
        compiler_params=_params(("arbitrary", "arbitrary", "arbitrary"), 40 * MIB),
        name="ssd_prompt",
    )(xs, bc, bc, dtraw, z, nrm.reshape(1, d_inner), dtb.reshape(1, heads), alog.reshape(1, heads),
      dsk_c.reshape(1, d_inner), onehot)


def _ssd_sample_kernel(dtb_ref, alog_ref, dsk_ref, h0_ref, dtt_ref, xst_ref, bm_ref, bmt_ref, cmt_ref,
                       *rest, ts):
    hn_ref, yt_ref = rest[-2:]
    k = pl.program_id(0)
    p = SSD_HEAD_DIM
    dt = _softplus(dtt_ref[pl.ds(k, 1), :] + dtb_ref[k])
    a = -jnp.exp(jnp.full((1, ts), alog_ref[k], F32))
    dec_t = jnp.broadcast_to(jnp.exp(dt * a), (p, ts))
    xs_t = xst_ref[...]
    xdt_t = xs_t * dt
    for j in range(ts):
        hn_ref[j] = h0_ref[j] * dec_t[:, j:j + 1] + xdt_t[:, j:j + 1] * bm_ref[j:j + 1, :]
    h0 = h0_ref[...]
    cm_t = cmt_ref[...]
    res = jnp.dot(h0.reshape(ts * p, h0.shape[2]).astype(BF16), cm_t.astype(BF16),
                  preferred_element_type=F32).reshape(ts, p, ts)
    jj = lax.broadcasted_iota(jnp.int32, res.shape, 0)
    ll = lax.broadcasted_iota(jnp.int32, res.shape, 2)
    y_off = jnp.sum(jnp.where(jj == ll, res, 0.0), axis=0)
    cb = jnp.sum(cm_t * bmt_ref[...], axis=0, keepdims=True)
    yt_ref[...] = cb * xdt_t + y_off * dec_t + dsk_ref[k] * xs_t


def ssd_sample(h0_all, layer, hn_prev, dtraw_t, xs_t, bm, bm_t, cm_t, dtb, alog, dsk, *, hpg):
    depth, ts, rows, nstate = h0_all.shape
    p = SSD_HEAD_DIM
    heads = rows // p
    smem = pl.BlockSpec(memory_space=pltpu.SMEM)
    in_specs = [smem, smem, smem,
                pl.BlockSpec((None, ts, p, nstate), lambda k: (layer, 0, k, 0)),
                pl.BlockSpec((heads, ts), lambda k: (0, 0)),
                pl.BlockSpec((p, ts), lambda k: (k, 0)),
                pl.BlockSpec((None, ts, nstate), lambda k: (k // hpg, 0, 0)),
                pl.BlockSpec((None, nstate, ts), lambda k: (k // hpg, 0, 0)),
                pl.BlockSpec((None, nstate, ts), lambda k: (k // hpg, 0, 0))]
    args = [dtb, alog, dsk, h0_all, dtraw_t, xs_t, bm, bm_t, cm_t]
    aliases = {}
    if hn_prev is not None:
        in_specs.append(pl.BlockSpec(memory_space=pl.ANY))
        args.append(hn_prev)
        aliases = {len(args) - 1: 0}
    return pl.pallas_call(
        functools.partial(_ssd_sample_kernel, ts=ts),
        grid=(heads,),
        in_specs=in_specs,
        out_specs=[pl.BlockSpec((None, ts, p, nstate), lambda k: (layer, 0, k, 0)),
                   pl.BlockSpec((p, ts), lambda k: (k, 0))],
        out_shape=[jax.ShapeDtypeStruct((depth, ts, rows, nstate), F32),
                   jax.ShapeDtypeStruct((rows, ts), F32)],
        input_output_aliases=aliases,
        compiler_params=_params(("arbitrary",), 40 * MIB),
        name="ssd_sample",
    )(*args)


def _gnorm_kernel(y_ref, z_ref, nrm_ref, yn_in_ref, o_ref):
    del yn_in_ref
    v = y_ref[...] * _silu(z_ref[...])
    v = v * lax.rsqrt(jnp.mean(v * v, axis=-1, keepdims=True) + EPS)
    o_ref[...] = (v * nrm_ref[...]).astype(o_ref.dtype)


def gated_norm_sample(y_s, z, nrm, yn, *, tp, groups):
    ts, d_inner = y_s.shape
    gc = d_inner // groups
    assert tp % ts == 0
    rb = tp // ts
    return pl.pallas_call(
        _gnorm_kernel,
        grid=(groups,),
        in_specs=[pl.BlockSpec((ts, gc), lambda g: (0, g)),
                  pl.BlockSpec((ts, gc), lambda g: (rb, g)),
                  pl.BlockSpec((1, gc), lambda g: (0, g)),
                  pl.BlockSpec(memory_space=pl.ANY)],
        out_specs=pl.BlockSpec((ts, gc), lambda g: (rb, g)),
        out_shape=jax.ShapeDtypeStruct(yn.shape, yn.dtype),
        input_output_aliases={3: 0},
        compiler_params=_params(("arbitrary",), 16 * MIB),
        name="gated_norm_sample",
    )(y_s, z, nrm.reshape(1, d_inner), yn)


def _attn_kernel(q_ref, k_ref, v_ref, o_ref, *, nh, hd):
    q = q_ref[...]
    k = k_ref[...].astype(BF16)
    v = v_ref[...].astype(BF16)
    scale = hd ** -0.5
    for h in range(nh):
        sl = slice(h * hd, (h + 1) * hd)
        s = lax.dot_general(q[:, sl], k[:, sl], (((1,), (1,)), ((), ())),
                            preferred_element_type=F32) * scale
        e = jnp.exp(s - jnp.max(s, axis=-1, keepdims=True))
        pr = e / jnp.sum(e, axis=-1, keepdims=True)
        o = jnp.dot(pr.astype(BF16), v[:, sl], preferred_element_type=F32)
        o_ref[:, sl] = o.astype(o_ref.dtype)


def _attn_cache_kernel(q_ref, k_ref, v_ref, o_ref, *, nh, hd):
    q = q_ref[...]
    scale = hd ** -0.5
    for h in range(nh):
        sl = slice(h * hd, (h + 1) * hd)
        kh = k_ref[:, h, :].astype(BF16)
        vh = v_ref[:, h, :].astype(BF16)
        s = lax.dot_general(q[:, sl], kh, (((1,), (1,)), ((), ())), preferred_element_type=F32) * scale
        e = jnp.exp(s - jnp.max(s, axis=-1, keepdims=True))
        pr = e / jnp.sum(e, axis=-1, keepdims=True)
        o = jnp.dot(pr.astype(BF16), vh, preferred_element_type=F32)
        o_ref[:, sl] = o.astype(o_ref.dtype)


def attention_cache(q, k_all, v_all, layer):
    b, lq, w = q.shape
    _, _, m, nh, hd = k_all.shape
    kv_spec = pl.BlockSpec((None, None, m, nh, hd), lambda bi: (layer, bi, 0, 0, 0))
    return pl.pallas_call(
        functools.partial(_attn_cache_kernel, nh=nh, hd=hd),
        grid=(b,),
        in_specs=[pl.BlockSpec((None, lq, w), lambda bi: (bi, 0, 0)), kv_spec, kv_spec],
        out_specs=pl.BlockSpec((None, lq, w), lambda bi: (bi, 0, 0)),
        out_shape=jax.ShapeDtypeStruct((b, lq, w), BF16),
        compiler_params=_params(("arbitrary",), 24 * MIB),
        name="attention_cache",
    )(q, k_all, v_all)


def attention(q, k, v, *, nh, hd):
    b, lq, w = q.shape
    m = k.shape[1]
    tq = _tile(lq, 512, 8)
    return pl.pallas_call(
        functools.partial(_attn_kernel, nh=nh, hd=hd),
        grid=(b, lq // tq),
        in_specs=[pl.BlockSpec((None, tq, w), lambda bi, i: (bi, i, 0)),
                  pl.BlockSpec((None, m, w), lambda bi, i: (bi, 0, 0)),
                  pl.BlockSpec((None, m, w), lambda bi, i: (bi, 0, 0))],
        out_specs=pl.BlockSpec((None, tq, w), lambda bi, i: (bi, i, 0)),
        out_shape=jax.ShapeDtypeStruct((b, lq, w), BF16),
        compiler_params=_params(("arbitrary", "arbitrary"), 24 * MIB),
        name="attention",
    )(q, k, v)


def _first_argmax(vals):
    best, arg = vals[0], jnp.zeros(vals[0].shape, jnp.int32)
    for i in range(1, len(vals)):
        better = vals[i] > best
        best = jnp.where(better, vals[i], best)
        arg = jnp.where(better, i, arg)
    return best, arg


def _pick(vals, arg):
    out = vals[0]
    for i in range(1, len(vals)):
        out = jnp.where(arg == i, vals[i], out)
    return out


def _router_kernel(x_ref, g_ref, wt_ref, bias_ref, xn_ref, idx_ref, wts_ref, *, n_experts):
    x = x_ref[...]
    xn = (x * lax.rsqrt(jnp.mean(x * x, axis=-1, keepdims=True) + EPS) * g_ref[...]).astype(BF16)
    xn_ref[...] = xn
    logits = lax.dot_general(wt_ref[...], xn, (((1,), (1,)), ((), ())), preferred_element_type=F32)
    scores = jax.nn.sigmoid(logits)
    biased = scores + bias_ref[...]
    per = n_experts // N_EXPERT_GROUPS
    assert per == 4 and TOP_K == 2
    row = lambda a, e: a[e:e + 1, :]
    gscore = []
    for gi in range(N_EXPERT_GROUPS):
        v = [row(biased, gi * per + i) for i in range(per)]
        a, b = jnp.maximum(v[0], v[1]), jnp.minimum(v[0], v[1])
        c, d = jnp.maximum(v[2], v[3]), jnp.minimum(v[2], v[3])
        gscore.append(jnp.maximum(a, c) + jnp.maximum(jnp.minimum(a, c), jnp.maximum(b, d)))
    _, grp = _first_argmax(gscore)
    bsel = [_pick([row(biased, gi * per + i) for gi in range(N_EXPERT_GROUPS)], grp) for i in range(per)]
    ssel = [_pick([row(scores, gi * per + i) for gi in range(N_EXPERT_GROUPS)], grp) for i in range(per)]
    _, i1 = _first_argmax(bsel)
    _, i2 = _first_argmax([jnp.where(i1 == i, -jnp.inf, bsel[i]) for i in range(per)])
    s1, s2 = _pick(ssel, i1), _pick(ssel, i2)
    tot = s1 + s2
    idx_ref[...] = jnp.concatenate([grp * per + i1, grp * per + i2], axis=0)
    wts_ref[...] = jnp.concatenate([s1 / tot, s2 / tot], axis=0)


def router(x, g, w_router, router_bias):
    t, d = x.shape
    e = w_router.shape[1]
    tr = _tile(t, 640, 128)
    return pl.pallas_call(
        functools.partial(_router_kernel, n_experts=e),
        grid=(t // tr,),
        in_specs=[pl.BlockSpec((tr, d), lambda i: (i, 0)),
                  pl.BlockSpec((1, d), lambda i: (0, 0)),
                  pl.BlockSpec((e, d), lambda i: (0, 0)),
                  pl.BlockSpec((e, 1), lambda i: (0, 0))],
        out_specs=[pl.BlockSpec((tr, d), lambda i: (i, 0)),
                   pl.BlockSpec((TOP_K, tr), lambda i: (0, i)),
                   pl.BlockSpec((TOP_K, tr), lambda i: (0, i))],
        out_shape=[jax.ShapeDtypeStruct((t, d), BF16),
                   jax.ShapeDtypeStruct((TOP_K, t), jnp.int32),
                   jax.ShapeDtypeStruct((TOP_K, t), F32)],
        compiler_params=_params(("arbitrary",), 8 * tr * d * 4 + 8 * MIB),
        name="router",
    )(x, g.reshape(1, d), w_router.T.astype(BF16), router_bias.astype(F32).reshape(e, 1))


def _moe_up_kernel(te_ref, tf_ref, nt_ref, x_ref, rw_ref, wg_ref, wu_ref, h_ref, wgb_ref, wub_ref):
    t = pl.program_id(1)

    def finish(gate, up):
        h_ref[...] = (_silu(gate) * up * rw_ref[...]).astype(h_ref.dtype)

    active = t < nt_ref[0]
    first = tf_ref[t] == 1
    chunks = max(1, wgb_ref.shape[0] // 512)

    @pl.when(active & first)
    def _():
        finish(_cast_dot(x_ref, wg_ref, wgb_ref, chunks), _cast_dot(x_ref, wu_ref, wub_ref, chunks))

    @pl.when(active & jnp.logical_not(first))
    def _():
        x = x_ref[...]
        finish(jnp.dot(x, wgb_ref[...], preferred_element_type=F32),
               jnp.dot(x, wub_ref[...], preferred_element_type=F32))

    @pl.when(jnp.logical_not(active))
    def _():
        h_ref[...] = jnp.zeros_like(h_ref)


def moe_up(x_sorted, row_w, w_gate, w_up, layer, tile_e, tile_first, n_tiles):
    pr, d = x_sorted.shape
    f = w_gate.shape[-1]
    tf = _tile(f, 512, 128)
    ntile = pr // MOE_TILE

    def tix(t, nt):
        return jnp.minimum(t, nt[0] - 1)

    grid_spec = pltpu.PrefetchScalarGridSpec(
        num_scalar_prefetch=3,
        grid=(f // tf, ntile),
        in_specs=[pl.BlockSpec((MOE_TILE, d), lambda fb, t, te, tfi, nt: (tix(t, nt), 0)),
                  pl.BlockSpec((MOE_TILE, 1), lambda fb, t, te, tfi, nt: (tix(t, nt), 0)),
                  pl.BlockSpec((None, None, d, tf), lambda fb, t, te, tfi, nt: (layer, te[tix(t, nt)], 0, fb)),
                  pl.BlockSpec((None, None, d, tf), lambda fb, t, te, tfi, nt: (layer, te[tix(t, nt)], 0, fb))],
        out_specs=pl.BlockSpec((MOE_TILE, tf), lambda fb, t, te, tfi, nt: (t, fb)),
        scratch_shapes=[pltpu.VMEM((d, tf), BF16), pltpu.VMEM((d, tf), BF16)],
    )
    vmem = 4 * d * tf * 4 + 2 * d * tf * 2 + 2 * MOE_TILE * d * 2 + 8 * MOE_TILE * tf * 4 + 2 * MIB
    return pl.pallas_call(
        _moe_up_kernel,
        grid_spec=grid_spec,
        out_shape=jax.ShapeDtypeStruct((pr, f), BF16),
        compiler_params=_params(("arbitrary", "arbitrary"), vmem),
        name="moe_up",
    )(tile_e, tile_first, n_tiles, x_sorted, row_w, w_gate, w_up)


def _moe_down_kernel(te_ref, tf_ref, nt_ref, h_ref, wd_ref, o_ref, wdb_ref):
    t = pl.program_id(1)

    active = t < nt_ref[0]
    first = tf_ref[t] == 1

    @pl.when(active & first)
    def _():
        o_ref[...] = _cast_dot(h_ref, wd_ref, wdb_ref, max(1, wdb_ref.shape[0] // 256))

    @pl.when(active & jnp.logical_not(first))
    def _():
        o_ref[...] = jnp.dot(h_ref[...], wdb_ref[...], preferred_element_type=F32)

    @pl.when(jnp.logical_not(active))
    def _():
        o_ref[...] = jnp.zeros_like(o_ref)


def moe_down(h_sorted, w_down, layer, tile_e, tile_first, n_tiles):
    pr, f = h_sorted.shape
    d = w_down.shape[-1]
    tn = _tile(d, 4096, 128)
    ntile = pr // MOE_TILE

    def tix(t, nt):
        return jnp.minimum(t, nt[0] - 1)

    grid_spec = pltpu.PrefetchScalarGridSpec(
        num_scalar_prefetch=3,
        grid=(d // tn, ntile),
        in_specs=[pl.BlockSpec((MOE_TILE, f), lambda nb, t, te, tfi, nt: (tix(t, nt), 0)),
                  pl.BlockSpec((None, None, f, tn), lambda nb, t, te, tfi, nt: (layer, te[tix(t, nt)], 0, nb))],
        out_specs=pl.BlockSpec((MOE_TILE, tn), lambda nb, t, te, tfi, nt: (t, nb)),
        scratch_shapes=[pltpu.VMEM((f, tn), BF16)],
    )
    vmem = 2 * f * tn * 4 + f * tn * 2 + 2 * MOE_TILE * f * 2 + 4 * MOE_TILE * tn * 4 + 2 * MIB
    return pl.pallas_call(
        _moe_down_kernel,
        grid_spec=grid_spec,
        out_shape=jax.ShapeDtypeStruct((pr, d), F32),
        compiler_params=_params(("arbitrary", "arbitrary"), vmem),
        name="moe_down",
    )(tile_e, tile_first, n_tiles, h_sorted, w_down)


def _dispatch_kernel(e_ref, pos_ref, tend_ref, cnt_ref, run_ref, tstart_ref, *, rows):
    ph = pl.program_id(0)
    i = pl.program_id(1)
    lanes = cnt_ref.shape[1]
    onehot = (e_ref[...] == lax.broadcasted_iota(jnp.int32, (rows, lanes), 1)).astype(F32)
    colsum = jnp.sum(onehot, axis=0, keepdims=True)

    @pl.when((ph == 0) & (i == 0))
    def _():
        cnt_ref[...] = jnp.zeros_like(cnt_ref)
        run_ref[...] = jnp.zeros_like(run_ref)

    @pl.when(ph == 0)
    def _():
        cnt_ref[...] += colsum

    @pl.when((ph == 1) & (i == 0))
    def _():
        tiles = jnp.floor((cnt_ref[...] + (MOE_TILE - 1)) * (1.0 / MOE_TILE))
        r = lax.broadcasted_iota(jnp.int32, (lanes, lanes), 0)
        c = lax.broadcasted_iota(jnp.int32, (lanes, lanes), 1)
        before = (r < c).astype(BF16)
        start = jnp.dot(jnp.broadcast_to(tiles, (8, lanes)).astype(BF16), before,
                        preferred_element_type=F32)[0:1]
        tstart_ref[...] = start
        tend_ref[...] = (start + tiles).astype(jnp.int32)

    @pl.when(ph == 1)
    def _():
        r = lax.broadcasted_iota(jnp.int32, (rows, rows), 0)
        c = lax.broadcasted_iota(jnp.int32, (rows, rows), 1)
        earlier = (c < r).astype(BF16)
        rank = jnp.dot(earlier, onehot.astype(BF16), preferred_element_type=F32) + run_ref[...]
        row = tstart_ref[...] * MOE_TILE + rank
        pos_ref[...] = jnp.sum(onehot * row, axis=1, keepdims=True).astype(jnp.int32)
        run_ref[...] += colsum


def dispatch(e_col, n_experts):
    na = e_col.shape[0]
    lanes = -(-n_experts // 128) * 128
    rows = _tile(na, 640, 128)
    assert rows % 128 == 0 and na + n_experts * MOE_TILE < 2 ** 24 and na // MOE_TILE + n_experts <= 256
    nblk = na // rows
    return pl.pallas_call(
        functools.partial(_dispatch_kernel, rows=rows),
        grid=(2, nblk),
        in_specs=[pl.BlockSpec((rows, 1), lambda ph, i: (i, 0))],
        out_specs=[pl.BlockSpec((rows, 1), lambda ph, i: (ph * i, 0)),
                   pl.BlockSpec((1, lanes), lambda ph, i: (0, 0))],
        out_shape=[jax.ShapeDtypeStruct((na, 1), jnp.int32),
                   jax.ShapeDtypeStruct((1, lanes), jnp.int32)],
        scratch_shapes=[pltpu.VMEM((1, lanes), F32), pltpu.VMEM((1, lanes), F32),
                        pltpu.VMEM((1, lanes), F32)],
        compiler_params=_params(("arbitrary", "arbitrary"), 16 * MIB),
        name="dispatch",
    )(e_col)


def moe(x, xn, idx_t, wts_t, w_gate, w_up, w_down, layer):
    t, d = x.shape
    n_experts = w_gate.shape[1]
    na = t * TOP_K
    ntile = -(-na // MOE_TILE) + n_experts
    pos_col, tile_end = dispatch(idx_t.reshape(na, 1), n_experts)
    pos = pos_col.reshape(TOP_K, t)
    tile_end = tile_end[0, :n_experts]
    tok = jnp.tile(jnp.arange(t, dtype=jnp.int32), TOP_K)
    row_token = jnp.zeros((ntile * MOE_TILE,), jnp.int32).at[pos_col[:, 0]].set(tok)
    row_w = jnp.zeros((ntile * MOE_TILE, 1), F32).at[pos_col[:, 0], 0].set(wts_t.reshape(-1))
    n_tiles = tile_end[-1:]
    tile_ids = jnp.arange(ntile, dtype=jnp.int32)
    tile_e = jnp.minimum(jnp.searchsorted(tile_end, tile_ids, side="right"), n_experts - 1).astype(jnp.int32)
    tile_first = jnp.concatenate([jnp.ones((1,), jnp.int32),
                                  (tile_e[1:] != tile_e[:-1]).astype(jnp.int32)])

    gather = lambda a, ix: a.at[ix].get(mode="promise_in_bounds")
    x_sorted = gather(xn, row_token)
    h_sorted = moe_up(x_sorted, row_w, w_gate, w_up, layer, tile_e, tile_first, n_tiles)
    o_sorted = moe_down(h_sorted, w_down, layer, tile_e, tile_first, n_tiles)
    y = gather(o_sorted, pos[0])
    for kk in range(1, TOP_K):
        y = y + gather(o_sorted, pos[kk])
    return x + y


def kernel(x_prompt, x_sample, mem_prompt, cache_mem_k, cache_mem_v, state_sconv, state_ssm_conv, state_ssm,
           norm_mix, w_in, sconv_w, w_sconv_out, ssd_conv_w, ssd_conv_b, ssd_dt_bias, ssd_a_log, ssd_d, ssd_norm,
           w_ssd_out, w_mix_out, norm_xattn, norm_mem, w_xk, w_xv, w_xq, w_xo, norm_moe, w_router, router_bias,
           w_gate, w_up, w_down, norm_final):
    nb, seq, d = x_prompt.shape
    ts = x_sample.shape[0]
    depth = w_in.shape[0]
    tp = nb * seq
    t = tp + ts
    c = sconv_w.shape[-1]
    d_inner = ssd_norm.shape[-1]
    heads = ssd_dt_bias.shape[-1]
    nstate = state_ssm.shape[-1]
    ch = ssd_conv_w.shape[-1]
    groups = (ch - d_inner) // (2 * nstate)
    hpg = heads // groups
    n_mem, nh, hd = cache_mem_k.shape[2:]
    xw = nh * hd
    assert x_sample.shape[1] == 1 and c == d and d_inner == 2 * d and heads * SSD_HEAD_DIM == d_inner
    assert ts <= seq and tp % ts == 0 and heads % 128 == 0

    col_z = 3 * c
    col_xbc = col_z + d_inner
    col_dt = col_xbc + ch
    col_g = col_dt + heads

    tm = _tile(t, 1040, 16)
    tn = 512
    mmk = functools.partial(mm, k=d, tm=tm)
    mmb = functools.partial(mm, k=d, tm=_tile(t, 1664, 16))
    sig = lambda acc: jax.nn.sigmoid(acc)

    onehot = (jnp.arange(heads)[:, None] == (jnp.arange(d_inner) // SSD_HEAD_DIM)[None, :]).astype(BF16)
    mem2 = mem_prompt.reshape(nb * n_mem, d)

    x = jnp.concatenate([x_prompt.reshape(tp, d), x_sample.reshape(ts, d)], axis=0)
    outs = {n: [] for n in ("mk", "mv", "scp", "ccp", "hp", "scs", "ccs")}
    hs = None

    for l in range(depth):
        xn = rmsnorm(x, norm_mix[l], BF16)
        c3 = mmb(xn, w_in, (l,), col0=0, ncols=3 * c, tn=tn, name="in_conv")
        z = mmb(xn, w_in, (l,), col0=col_z, ncols=d_inner, tn=tn, name="in_z")
        xraw = mmb(xn, w_in, (l,), col0=col_xbc, ncols=ch, tn=tn, name="in_xbc")
        dtraw = mmk(xn, w_in, (l,), col0=col_dt, ncols=heads, tn=heads, name="in_dt")
        gates = mmb(xn, w_in, (l,), col0=col_g, ncols=2 * d, tn=tn, epilogue=sig, name="in_gates")

        sc_state = jnp.moveaxis(state_sconv[l], 1, 0)
        v, scp, u_new = sconv(c3, sconv_w[l], sc_state, nb=nb, seq=seq, ts=ts)
        cc_state = jnp.moveaxis(state_ssm_conv[l], 1, 0)
        cargs = dict(nb=nb, seq=seq, ts=ts)
        xs, ccp_x, xs_s = xconv(xraw, ssd_conv_w[l], ssd_conv_b[l], cc_state, col0=0, ncols=d_inner,
                                out_dtype=F32, **cargs)
        bc, ccp_bc, bc_s = xconv(xraw, ssd_conv_w[l], ssd_conv_b[l], cc_state, col0=d_inner,
                                 ncols=ch - d_inner, out_dtype=BF16, **cargs)
        ccp = jnp.concatenate([ccp_x, ccp_bc], axis=-1)

        dsk_c = jnp.repeat(ssd_d[l], SSD_HEAD_DIM)
        yn, hp = ssd_prompt(xs, bc, dtraw, z, ssd_norm[l], ssd_dt_bias[l], ssd_a_log[l], dsk_c, onehot,
                            nb=nb, seq=seq, t=t, d_inner=d_inner, groups=groups, nstate=nstate)

        bm_s = bc_s[:, :groups * nstate].reshape(ts, groups, nstate)
        cm_s = bc_s[:, groups * nstate:].reshape(ts, groups, nstate)
        hs, y_t = ssd_sample(state_ssm.reshape(depth, ts, d_inner, nstate), l, hs, dtraw[tp:].T, xs_s.T,
                             jnp.moveaxis(bm_s, 1, 0), jnp.transpose(bm_s, (1, 2, 0)),
                             jnp.transpose(cm_s, (1, 2, 0)),
                             ssd_dt_bias[l], ssd_a_log[l], ssd_d[l], hpg=hpg)
        yn = gated_norm_sample(y_t.T, z, ssd_norm[l], yn, tp=tp, groups=groups)

        y_conv = mmk(v, w_sconv_out, (l,), ncols=d, tn=tn, extras=[(gates, 0)],
                     epilogue=lambda acc, gc_: acc * gc_, name="sconv_out")
        y_ssd0 = mmb(yn, w_ssd_out, (l,), ncols=d, tn=tn, kb_a=0, kb_w=0, name="ssd_out0")
        mixed = mmk(yn, w_ssd_out, (l,), ncols=d, tn=tn, kb_a=1, kb_w=1,
                    extras=[(y_ssd0, 0), (y_conv, 0), (gates, d // tn)],
                    epilogue=lambda acc, y0, yc, gs: yc + gs * (y0 + acc), out_dtype=BF16, name="ssd_out1")
        x = mmk(mixed, w_mix_out, (l,), ncols=d, tn=tn, extras=[(x, 0)],
                epilogue=lambda acc, xr: xr + acc, name="mix_out")

        memn = rmsnorm(mem2, norm_mem[l], BF16)
        tmm = _tile(nb * n_mem, 1024, 16)
        k_p = mm(memn, w_xk, (l,), k=d, ncols=xw, tn=tn, tm=tmm, name="mem_k")
        v_p = mm(memn, w_xv, (l,), k=d, ncols=xw, tn=tn, tm=tmm, name="mem_v")
        xn = rmsnorm(x, norm_xattn[l], BF16)
        qa = mmk(xn, w_xq, (l,), ncols=xw, tn=tn, out_dtype=BF16, name="xq")
        o_p = attention(qa[:tp].reshape(nb, seq, xw), k_p.reshape(nb, n_mem, xw), v_p.reshape(nb, n_mem, xw),
                        nh=nh, hd=hd)
        q_s = jnp.broadcast_to(qa[tp:].reshape(ts, 1, xw), (ts, 8, xw))
        o_s = attention_cache(q_s, cache_mem_k, cache_mem_v, l)
        o = jnp.concatenate([o_p.reshape(tp, xw), o_s[:, 0]], axis=0)
        x = mm(o, w_xo, (l,), k=xw, ncols=d, tn=1024, tm=tm, extras=[(x, 0)],
               epilogue=lambda acc, xr: xr + acc, name="xo")

        xn, idx_t, wts_t = router(x, norm_moe[l], w_router, router_bias)
        x = moe(x, xn, idx_t, wts_t, w_gate, w_up, w_down, l)

        outs["mk"].append(k_p.reshape(nb, n_mem, nh, hd))
        outs["mv"].append(v_p.reshape(nb, n_mem, nh, hd))
        outs["scp"].append(scp)
        outs["ccp"].append(ccp)
        outs["hp"].append(hp)
        outs["scs"].append(jnp.stack([state_sconv[l][:, 1], u_new], axis=1))
        outs["ccs"].append(jnp.concatenate([state_ssm_conv[l][:, 1:], xraw[tp:, None, :]], axis=1))

    y_p = rmsnorm(x, norm_final, F32, row0=0, nrows=tp)
    y_s = rmsnorm(x, norm_final, F32, row0=tp, nrows=ts)
    st = lambda n: jnp.stack(outs[n])
    return (y_p.reshape(nb, seq, d), y_s.reshape(ts, 1, d), st("mk"), st("mv"), st("scp"), st("ccp"),
            st("hp"), st("scs"), st("ccs"), hs.reshape(depth, ts, heads, SSD_HEAD_DIM, nstate))
```
